```python
import math
import jax
import jax.numpy as jnp
from jax import lax
import numpy as np

D_MODEL = 2048
BATCH = 32
SEQ = 256
DEPTH = 4
DEC_BATCH = 8
DEC_SEQ = 1024
PAST_LEN = 256

GRID_W = 64
EPS = 1e-6

HG_WIDTH = D_MODEL // 2
HG_DK = 128
HG_HEADS = HG_WIDTH // HG_DK
HG_DV = HG_WIDTH // HG_HEADS
HG_CHUNK = 32

AT_HD = 128
AT_Q_HEADS = (D_MODEL // 2) // AT_HD
AT_KV_HEADS = AT_Q_HEADS // 4
AT_GROUP = AT_Q_HEADS // AT_KV_HEADS
AT_WIDTH = AT_Q_HEADS * AT_HD
AT_KV_WIDTH = AT_KV_HEADS * AT_HD
Q_BLOCK = 128
ROPE_THETA = 10000.0
ROPE_AX = AT_HD // 2

S5_WIDTH = D_MODEL // 2
S5_GROUP_CH = 16
S5_GROUPS = S5_WIDTH // S5_GROUP_CH
S5_STATE = 64

IN_COLS = 5 * HG_WIDTH + 2 * AT_WIDTH + 2 * AT_KV_WIDTH + 2 * S5_WIDTH + 3 * D_MODEL

kernel_name = 'hybrid_hgrn2_gqa_s5_diffusion_step'


def rms_norm(x, w):
    xf = x.astype(jnp.float32)
    y = xf * lax.rsqrt(jnp.mean(xf * xf, axis=-1, keepdims=True) + EPS)
    return (y * w.astype(jnp.float32)).astype(x.dtype)


def split_columns(z):
    sizes = (HG_WIDTH,) * 5 + (AT_WIDTH, AT_KV_WIDTH, AT_KV_WIDTH, AT_WIDTH, S5_WIDTH, S5_WIDTH, 3 * D_MODEL)
    offsets = []
    acc = 0
    for s in sizes[:-1]:
        acc += s
        offsets.append(acc)
    return jnp.split(z, offsets, axis=-1)


def hgrn_lower_bounds(logits):
    p = jax.nn.softmax(logits.astype(jnp.float32), axis=1)
    cs = jnp.cumsum(p, axis=1)
    return cs - cs[:, :1]


def hgrn_forget(f_raw, lb):
    bsz, seq, _ = f_raw.shape
    lb = lb.astype(jnp.float32)
    f = lb + (1.0 - lb) * jax.nn.sigmoid(f_raw.astype(jnp.float32))
    shape = (bsz, seq, HG_HEADS, HG_DK)
    return jnp.log(f).reshape(shape), (1.0 - f).reshape(shape)


def hgrn_chunk_scan(q, k, v, log_f, s0):
    bsz, seq, nh, _ = q.shape
    dv = v.shape[-1]
    n_chunks = seq // HG_CHUNK

    def chunks(a):
        return a.astype(jnp.float32).reshape(bsz, n_chunks, HG_CHUNK, nh, a.shape[-1]).transpose(1, 0, 3, 2, 4)

    causal = jnp.tril(jnp.ones((HG_CHUNK, HG_CHUNK), dtype=bool))[:, :, None]

    def step(state, blk):
        qc, kc, vc, gc = blk
        b = jnp.cumsum(gc, axis=2)
        rel = b[:, :, :, None, :] - b[:, :, None, :, :]
        decay = jnp.exp(jnp.where(causal, rel, -jnp.inf))
        scores = jnp.einsum('bhtd,bhtsd,bhsd->bhts', qc, decay, kc)
        o = jnp.einsum('bhts,bhsv->bhtv', scores, vc) + jnp.einsum('bhtd,bhdv->bhtv', qc * jnp.exp(b), state)
        b_last = b[:, :, -1:, :]
        new_state = jnp.exp(b_last[:, :, 0, :])[..., None] * state + jnp.einsum(
            'bhsd,bhsv->bhdv', kc * jnp.exp(b_last - b), vc)
        return new_state, o

    s_final, o = lax.scan(step, s0.astype(jnp.float32), (chunks(q), chunks(k), chunks(v), chunks(log_f)))
    o = o.transpose(1, 0, 3, 2, 4).reshape(bsz, seq, nh, dv)
    return o, s_final


def hgrn_branch(q_raw, i_raw, ff_raw, fb_raw, g_raw, lb_f, lb_b, onorm_w, s0):
    bsz, seq, _ = q_raw.shape
    q = q_raw.reshape(bsz, seq, HG_HEADS, HG_DK)
    v = i_raw.reshape(bsz, seq, HG_HEADS, HG_DV)
    logf_f, k_f = hgrn_forget(ff_raw, lb_f)
    logf_b, k_b = hgrn_forget(fb_raw, lb_b)
    o_f, s_f = hgrn_chunk_scan(q, k_f, v, logf_f, s0[:, 0])
    o_b, s_b = hgrn_chunk_scan(jnp.flip(q, 1), jnp.flip(k_b, 1), jnp.flip(v, 1), jnp.flip(logf_b, 1), s0[:, 1])
    o = rms_norm(o_f + jnp.flip(o_b, 1), onorm_w).reshape(bsz, seq, HG_WIDTH).astype(q_raw.dtype)
    return o * jax.nn.silu(g_raw), jnp.stack([s_f, s_b], axis=1)


def axial_rope_tables(seq):
    rows = seq // GRID_W
    row = jnp.repeat(jnp.arange(rows, dtype=jnp.float32), GRID_W)
    col = jnp.tile(jnp.arange(GRID_W, dtype=jnp.float32), rows)
    half = ROPE_AX // 2
    inv = ROPE_THETA ** (-jnp.arange(half, dtype=jnp.float32) / half)
    ang_r = row[:, None] * inv
    ang_c = col[:, None] * inv
    return (jnp.cos(ang_r), jnp.sin(ang_r), jnp.cos(ang_c), jnp.sin(ang_c))


def rope_rotate(x, cos, sin):
    half = x.shape[-1] // 2
    x1, x2 = x[..., :half], x[..., half:]
    cos = cos[None, :, None, :]
    sin = sin[None, :, None, :]
    return jnp.concatenate([x1 * cos - x2 * sin, x2 * cos + x1 * sin], axis=-1)


def apply_axial_rope(x, tables):
    cos_r, sin_r, cos_c, sin_c = tables
    xf = x.astype(jnp.float32)
    y = jnp.concatenate([rope_rotate(xf[..., :ROPE_AX], cos_r, sin_r),
                         rope_rotate(xf[..., ROPE_AX:], cos_c, sin_c)], axis=-1)
    return y.astype(x.dtype)


def block_attention(q, k, v):
    bsz, tq = q.shape[:2]
    nb = tq // Q_BLOCK
    qb = q.reshape(bsz, nb, Q_BLOCK, AT_KV_HEADS, AT_GROUP, AT_HD).transpose(1, 0, 2, 3, 4, 5)
    scale = AT_HD ** -0.5

    def one_block(qblk):
        s = jnp.einsum('bqkgd,bskd->bkgqs', qblk, k).astype(jnp.float32) * scale
        p = jax.nn.softmax(s, axis=-1).astype(v.dtype)
        return jnp.einsum('bkgqs,bskd->bqkgd', p, v)

    o = lax.map(one_block, qb)
    return o.transpose(1, 0, 2, 3, 4, 5).reshape(bsz, tq, AT_WIDTH)


def cmul(ar, ai, br, bi):
    return ar * br - ai * bi, ar * bi + ai * br


def ssm_combine(e1, e2):
    a1r, a1i, b1r, b1i = e1
    a2r, a2i, b2r, b2i = e2
    ar, ai = cmul(a2r, a2i, a1r, a1i)
    br, bi = cmul(a2r, a2i, b1r, b1i)
    return ar, ai, br + b2r, bi + b2i


def s5_discretise(a_re, a_im, log_dt, b_re, b_im):
    a_re = a_re.astype(jnp.float32)
    a_im = a_im.astype(jnp.float32)
    dt = jnp.exp(log_dt.astype(jnp.float32))[:, None]
    mag = jnp.exp(a_re * dt)
    abar_re = mag * jnp.cos(a_im * dt)
    abar_im = mag * jnp.sin(a_im * dt)
    den = a_re * a_re + a_im * a_im
    num_re = abar_re - 1.0
    coef_re = (num_re * a_re + abar_im * a_im) / den
    coef_im = (abar_im * a_re - num_re * a_im) / den
    bbar_re, bbar_im = cmul(coef_re[..., None], coef_im[..., None],
                            b_re.astype(jnp.float32), b_im.astype(jnp.float32))
    return abar_re, abar_im, bbar_re, bbar_im


def s5_scan(u, abar_re, abar_im, bbar_re, bbar_im, s0_re, s0_im):
    bu_re = jnp.einsum('btgc,gpc->btgp', u, bbar_re)
    bu_im = jnp.einsum('btgc,gpc->btgp', u, bbar_im)
    shape = (1, u.shape[1]) + abar_re.shape
    a_re = jnp.broadcast_to(abar_re, shape)
    a_im = jnp.broadcast_to(abar_im, shape)
    cum_re, cum_im, s_re, s_im = lax.associative_scan(ssm_combine, (a_re, a_im, bu_re, bu_im), axis=1)
    init_re, init_im = cmul(cum_re, cum_im, s0_re[:, None].astype(jnp.float32), s0_im[:, None].astype(jnp.float32))
    return s_re + init_re, s_im + init_im


def s5_branch(u_raw, gate_raw, a_re, a_im, log_dt, b_re, b_im, c_re, c_im, d, w_glu, s0_re, s0_im):
    bsz, seq, _ = u_raw.shape
    u = u_raw.astype(jnp.float32).reshape(bsz, seq, S5_GROUPS, S5_GROUP_CH)
    fwd = s5_discretise(a_re[0], a_im[0], log_dt[0], b_re[0], b_im[0])
    bwd = s5_discretise(a_re[1], a_im[1], log_dt[1], b_re[1], b_im[1])
    sf_re, sf_im = s5_scan(u, *fwd, s0_re[:, 0], s0_im[:, 0])
    sb_re, sb_im = s5_scan(jnp.flip(u, 1), *bwd, s0_re[:, 1], s0_im[:, 1])
    final_re = jnp.stack([sf_re[:, -1], sb_re[:, -1]], axis=1)
    final_im = jnp.stack([sf_im[:, -1], sb_im[:, -1]], axis=1)
    s_re = sf_re + jnp.flip(sb_re, 1)
    s_im = sf_im + jnp.flip(sb_im, 1)
    y = jnp.einsum('btgp,gcp->btgc', s_re, c_re.astype(jnp.float32)) - jnp.einsum(
        'btgp,gcp->btgc', s_im, c_im.astype(jnp.float32))
    y = (y.reshape(bsz, seq, S5_WIDTH) + d.astype(jnp.float32) * u_raw.astype(jnp.float32)).astype(u_raw.dtype)
    y = jax.nn.gelu(y)
    y = y * jax.nn.sigmoid(y @ w_glu)
    return y * jax.nn.silu(gate_raw), final_re, final_im


def trunk_layer(x, mod, l, p, lower_bounds, hg_s0, s5_s0_re, s5_s0_im, ctx_k, ctx_v, rope):
    bsz, seq, _ = x.shape
    shift, scale, gate = jnp.split(mod, 3, axis=-1)
    h = rms_norm(x, p['norm_w'][l]) * (1 + scale[:, None, :]) + shift[:, None, :]
    (hg_q, hg_i, hg_ff, hg_fb, hg_g, at_q, at_k, at_v, at_g,
     s5_u, s5_g, merge_raw) = split_columns(h @ p['w_in'][l])

    o_hg, hg_state = hgrn_branch(hg_q, hg_i, hg_ff, hg_fb, hg_g, lower_bounds[0, l], lower_bounds[1, l],
                                 p['hg_onorm'][l], hg_s0)

    q = rms_norm(at_q.reshape(bsz, seq, AT_Q_HEADS, AT_HD), p['at_q_norm'][l])
    k = rms_norm(at_k.reshape(bsz, seq, AT_KV_HEADS, AT_HD), p['at_k_norm'][l])
    v = at_v.reshape(bsz, seq, AT_KV_HEADS, AT_HD)
    if ctx_k is None:
        o_at = block_attention(q, k, v)
    else:
        q_r = apply_axial_rope(q, rope)
        k_r = apply_axial_rope(k, rope)
        k_all = jnp.concatenate([ctx_k.astype(k.dtype), k_r], axis=1)
        v_all = jnp.concatenate([ctx_v.astype(v.dtype), v], axis=1)
        o_at = block_attention(q_r, k_all, v_all)
    o_at = o_at * jax.nn.silu(at_g)

    o_s5, s5_re, s5_im = s5_branch(s5_u, s5_g, p['s5_a_re'][l], p['s5_a_im'][l], p['s5_log_dt'][l],
                                   p['s5_b_re'][l], p['s5_b_im'][l], p['s5_c_re'][l], p['s5_c_im'][l],
                                   p['s5_d'][l], p['s5_w_glu'][l], s5_s0_re, s5_s0_im)

    g_hg, g_at, g_s5 = jnp.split(jax.nn.sigmoid(merge_raw), 3, axis=-1)
    merged = (g_hg * (o_hg @ p['w_br_hg'][l]) + g_at * (o_at @ p['w_br_at'][l])
              + g_s5 * (o_s5 @ p['w_br_s5'][l]))
    x = x + gate[:, None, :] * (merged @ p['w_out'][l])
    return x, (k, v, hg_state, s5_re, s5_im)


def setup_inputs(seed: int = 0) -> dict:
    key = jax.random.key(seed)
    ks = iter(jax.random.split(key, 40))

    def nrm(shape, s):
        return s * jax.random.normal(next(ks), shape, jnp.float32)

    G, P, CH = S5_GROUPS, S5_STATE, S5_GROUP_CH
    return {
        'x_prompt': nrm((BATCH, SEQ, D_MODEL), 1.0),
        'x_sample': nrm((DEC_BATCH, DEC_SEQ, D_MODEL), 1.0),
        'cache_k': nrm((DEC_BATCH, DEPTH, PAST_LEN, AT_KV_HEADS, AT_HD), 1.0),
        'cache_v': nrm((DEC_BATCH, DEPTH, PAST_LEN, AT_KV_HEADS, AT_HD), 1.0),
        'state_hgrn': nrm((DEC_BATCH, DEPTH, 2, HG_HEADS, HG_DK, HG_DV), 0.5),
        'state_s5_re': nrm((DEC_BATCH, DEPTH, 2, G, P), 0.5),
        'state_s5_im': nrm((DEC_BATCH, DEPTH, 2, G, P), 0.5),
        'c': nrm((DEC_BATCH, D_MODEL), 1.0),
        'c_ctx': nrm((D_MODEL,), 1.0),
        'norm_w': 1.0 + nrm((DEPTH, D_MODEL), 0.02),
        'w_mod': nrm((DEPTH, D_MODEL, 3 * D_MODEL), 0.5 * D_MODEL ** -0.5),
        'b_mod': nrm((DEPTH, 3 * D_MODEL), 0.02),
        'w_in': nrm((DEPTH, D_MODEL, IN_COLS), D_MODEL ** -0.5),
        'hg_lb_logits': nrm((2, DEPTH, HG_WIDTH), 0.1),
        'hg_onorm': 1.0 + nrm((DEPTH, HG_DV), 0.02),
        'at_q_norm': 1.0 + nrm((DEPTH, AT_HD), 0.02),
        'at_k_norm': 1.0 + nrm((DEPTH, AT_HD), 0.02),
        's5_a_re': -0.5 * jnp.exp(nrm((DEPTH, 2, G, P), 0.05)),
        's5_a_im': math.pi * jnp.arange(P, dtype=jnp.float32) + nrm((DEPTH, 2, G, P), 0.01),
        's5_log_dt': jax.random.uniform(next(ks), (DEPTH, 2, G), jnp.float32,
                                        minval=math.log(0.001), maxval=math.log(0.1)),
        's5_b_re': nrm((DEPTH, 2, G, P, CH), (2 * CH) ** -0.5),
        's5_b_im': nrm((DEPTH, 2, G, P, CH), (2 * CH) ** -0.5),
        's5_c_re': nrm((DEPTH, G, CH, P), (2 * P) ** -0.5),
        's5_c_im': nrm((DEPTH, G, CH, P), (2 * P) ** -0.5),
        's5_d': nrm((DEPTH, S5_WIDTH), 0.5),
        's5_w_glu': nrm((DEPTH, S5_WIDTH, S5_WIDTH), S5_WIDTH ** -0.5),
        'w_br_hg': nrm((DEPTH, HG_WIDTH, D_MODEL), HG_WIDTH ** -0.5),
        'w_br_at': nrm((DEPTH, AT_WIDTH, D_MODEL), AT_WIDTH ** -0.5),
        'w_br_s5': nrm((DEPTH, S5_WIDTH, D_MODEL), S5_WIDTH ** -0.5),
        'w_out': nrm((DEPTH, D_MODEL, D_MODEL), D_MODEL ** -0.5),
        'final_norm': 1.0 + nrm((D_MODEL,), 0.02),
    }


def reference(x_prompt, x_sample, cache_k, cache_v, state_hgrn, state_s5_re, state_s5_im, c, c_ctx,
              norm_w, w_mod, b_mod, w_in, hg_lb_logits, hg_onorm, at_q_norm, at_k_norm,
              s5_a_re, s5_a_im, s5_log_dt, s5_b_re, s5_b_im, s5_c_re, s5_c_im, s5_d, s5_w_glu,
              w_br_hg, w_br_at, w_br_s5, w_out, final_norm):
    p = {'norm_w': norm_w, 'w_in': w_in, 'hg_onorm': hg_onorm, 'at_q_norm': at_q_norm,
         'at_k_norm': at_k_norm, 's5_a_re': s5_a_re, 's5_a_im': s5_a_im, 's5_log_dt': s5_log_dt,
         's5_b_re': s5_b_re, 's5_b_im': s5_b_im, 's5_c_re': s5_c_re, 's5_c_im': s5_c_im,
         's5_d': s5_d, 's5_w_glu': s5_w_glu, 'w_br_hg': w_br_hg, 'w_br_at': w_br_at,
         'w_br_s5': w_br_s5, 'w_out': w_out}
    lower_bounds = hgrn_lower_bounds(hg_lb_logits)

    bsz_p = x_prompt.shape[0]
    hg_zero = jnp.zeros((bsz_p, 2, HG_HEADS, HG_DK, HG_DV), jnp.float32)
    s5_zero = jnp.zeros((bsz_p, 2, S5_GROUPS, S5_STATE), jnp.float32)
    c_ctx_silu = jax.nn.silu(c_ctx)[None, :]
    xp = x_prompt
    ks_l, vs_l, hs_l, sre_l, sim_l = [], [], [], [], []
    for l in range(DEPTH):
        mod = c_ctx_silu @ w_mod[l] + b_mod[l]
        xp, (k_l, v_l, h_l, sr_l, si_l) = trunk_layer(xp, mod, l, p, lower_bounds, hg_zero, s5_zero, s5_zero,
                                                      None, None, None)
        ks_l.append(k_l)
        vs_l.append(v_l)
        hs_l.append(h_l)
        sre_l.append(sr_l)
        sim_l.append(si_l)
    y_prompt = rms_norm(xp, final_norm)
    new_cache_k = jnp.stack(ks_l, axis=1)
    new_cache_v = jnp.stack(vs_l, axis=1)
    new_state_hgrn = jnp.stack(hs_l, axis=1)
    new_state_s5_re = jnp.stack(sre_l, axis=1)
    new_state_s5_im = jnp.stack(sim_l, axis=1)

    rope = axial_rope_tables(x_sample.shape[1])
    c_silu = jax.nn.silu(c)
    xs = x_sample
    for l in range(DEPTH):
        mod = c_silu @ w_mod[l] + b_mod[l]
        xs, _ = trunk_layer(xs, mod, l, p, lower_bounds, state_hgrn[:, l], state_s5_re[:, l], state_s5_im[:, l],
                            cache_k[:, l], cache_v[:, l], rope)
    y_sample = rms_norm(xs, final_norm)

    return (y_prompt, y_sample, new_cache_k, new_cache_v, new_state_hgrn, new_state_s5_re, new_state_s5_im)
```

```python
import functools
import math

import jax
import jax.numpy as jnp
from jax import lax
from jax.experimental import pallas as pl
from jax.experimental.pallas import tpu as pltpu

F32 = jnp.float32
BF16 = jnp.bfloat16
EPS = 1e-6

D_MODEL = 2048
HG_WIDTH = 1024
HEAD = 128
HG_HEADS = 8
AT_KV_HEADS = 2
AT_GROUP = 4
S5_GROUPS = 64
S5_STATE = 64
S5_GROUP_CH = 16
GRID_W = 64
ROPE_THETA = 10000.0

COL_MERGE = 0
COL_HG = 6144
COL_S5U = 11264
COL_S5G = 12288
COL_ATQ = 13312
COL_ATG = 14336
COL_ATK = 15360
COL_ATV = 15616
IN_COLS = 15872

TM_IN = 1024
TN_IN = 512
TM_OUT = 256
HG_CHUNK = 64
S5_GB = 16
S5_NBLK = S5_GROUPS // S5_GB
S5_ROWS = 1024
VMEM_LIMIT = 56 * 1024 * 1024


def _sigmoid(x):
    return 1.0 / (1.0 + jnp.exp(-x))


def _dot(a, b):
    return jnp.dot(a, b, preferred_element_type=F32)


def _dot_nt(a, b):
    return lax.dot_general(a, b, (((1,), (1,)), ((), ())), preferred_element_type=F32)


def _dot_tn(a, b):
    return lax.dot_general(a, b, (((0,), (0,)), ((), ())), preferred_element_type=F32)


def _rms(x, w):
    ms = jnp.mean(x * x, axis=-1, keepdims=True)
    return x * lax.rsqrt(ms + EPS) * w


def _mod_kernel(c_ref, w_ref, b_ref, o_ref):
    c = c_ref[...]
    s = (c * _sigmoid(c)).astype(BF16)
    o_ref[...] = _dot(s, w_ref[...].astype(BF16)) + b_ref[...]


def _modulation(cvec, w_mod, b_mod):
    depth, d, n = w_mod.shape
    tn = 1024
    return pl.pallas_call(
        _mod_kernel,
        grid=(depth, n // tn),
        in_specs=[pl.BlockSpec((16, d), lambda l, j: (0, 0)),
                  pl.BlockSpec((None, d, tn), lambda l, j: (l, 0, j)),
                  pl.BlockSpec((None, 1, tn), lambda l, j: (l, 0, j))],
        out_specs=pl.BlockSpec((None, 16, tn), lambda l, j: (l, 0, j)),
        out_shape=jax.ShapeDtypeStruct((depth, 16, n), F32),
        compiler_params=pltpu.CompilerParams(
            dimension_semantics=("arbitrary", "arbitrary"), vmem_limit_bytes=VMEM_LIMIT),
        name="modulation",
    )(cvec, w_mod, b_mod.reshape(depth, 1, n))


def _in_proj_kernel(x_ref, mod_ref, nw_ref, w_ref, z_ref, u_ref, hn_ref, *, seq, spt):
    j = pl.program_id(1)

    @pl.when(j == 0)
    def _():
        h = _rms(x_ref[...], nw_ref[...])
        h = h * (1.0 + mod_ref[1:2, :]) + mod_ref[0:1, :]
        hn_ref[...] = h.astype(BF16)

    res = _dot(hn_ref[...], w_ref[...])
    z_ref[...] = res

    for jj in range(HG_WIDTH // TN_IN):
        @pl.when(j == COL_S5U // TN_IN + jj)
        def _():
            for k in range(spt):
                u_ref[:, k * HG_WIDTH + jj * TN_IN:k * HG_WIDTH + (jj + 1) * TN_IN] = (
                    res[k * seq:(k + 1) * seq, :])


def _in_proj(x2, mod3, norm_w, w, *, seq, mod_row):
    n_tok, d = x2.shape
    nseq = n_tok // seq
    spt = TM_IN // seq
    return pl.pallas_call(
        functools.partial(_in_proj_kernel, seq=seq, spt=spt),
        grid=(n_tok // TM_IN, IN_COLS // TN_IN),
        in_specs=[pl.BlockSpec((TM_IN, d), lambda i, j: (i, 0)),
                  pl.BlockSpec((None, 3, d), lambda i, j: (mod_row(i), 0, 0)),
                  pl.BlockSpec((1, d), lambda i, j: (0, 0)),
                  pl.BlockSpec((d, TN_IN), lambda i, j: (0, j))],
        out_specs=[pl.BlockSpec((TM_IN, TN_IN), lambda i, j: (i, j)),
                   pl.BlockSpec((seq, spt * HG_WIDTH), lambda i, j: (0, i))],
        out_shape=[jax.ShapeDtypeStruct((n_tok, IN_COLS), F32),
                   jax.ShapeDtypeStruct((seq, nseq * HG_WIDTH), F32)],
        scratch_shapes=[pltpu.VMEM((TM_IN, d), BF16)],
        compiler_params=pltpu.CompilerParams(
            dimension_semantics=("arbitrary", "arbitrary"), vmem_limit_bytes=VMEM_LIMIT),
        name="in_proj",
    )(x2, mod3, norm_w, w)


def _ref_rows(b, bs, ridx, sub8):
    c = b.shape[0]
    if bs >= 8:
        b3 = b.reshape(c // bs, bs, HEAD)
        return jnp.broadcast_to(b3[:, ridx:ridx + 1, :], (c // bs, bs, HEAD)).reshape(c, HEAD)
    b3 = b.reshape(c // 8, 8, HEAD)
    r = None
    for blk in range(8 // bs - 1, -1, -1):
        row = jnp.broadcast_to(b3[:, blk * bs + ridx:blk * bs + ridx + 1, :], (c // 8, 8, HEAD))
        r = row if r is None else jnp.where(sub8 < (blk + 1) * bs, row, r)
    return r.reshape(c, HEAD)


def _cumsum3(tri, g):
    hi = g.astype(BF16)
    r1 = g - hi.astype(F32)
    mid = r1.astype(BF16)
    lo = (r1 - mid.astype(F32)).astype(BF16)
    return _dot(tri, hi) + _dot(tri, mid) + _dot(tri, lo)


def _hgrn_kernel(*refs, seq, has_s0, write_state):
    q_ref, v_ref, ff_ref, fb_ref, g_ref, lbf_ref, lbb_ref, onw_ref = refs[:8]
    pos = 8
    s0_ref = st_ref = None
    if has_s0:
        s0_ref = refs[pos]
        pos += 1
    o_ref = refs[pos]
    pos += 1
    if write_state:
        st_ref = refs[pos]
        pos += 1
    acc_ref = refs[pos]

    c = HG_CHUNK
    n = seq // c
    levels = int(math.log2(c))
    ti = lax.broadcasted_iota(jnp.int32, (c, c), 0)
    si = lax.broadcasted_iota(jnp.int32, (c, c), 1)
    xor = jnp.bitwise_xor(ti, si)
    sub8 = lax.broadcasted_iota(jnp.int32, (c // 8, 8, HEAD), 1)

    def run(rev):
        fr_ref = fb_ref if rev else ff_ref
        lb = (lbb_ref if rev else lbf_ref)[...]
        keep = (si >= ti) if rev else (si <= ti)
        tri = jnp.where(keep, 1.0, 0.0).astype(BF16)
        if has_s0:
            st0 = s0_ref[1 if rev else 0].T
        else:
            st0 = jnp.zeros((HEAD, HEAD), F32)

        def body(ci, st):
            cc = (n - 1 - ci) if rev else ci
            rows = pl.ds(pl.multiple_of(cc * c, c), c)
            q = q_ref[rows, :]
            v = v_ref[rows, :]
            f = lb + (1.0 - lb) * _sigmoid(fr_ref[rows, :])
            g = jnp.log(f)
            k = 1.0 - f
            b = _cumsum3(tri, g)
            vb = v.astype(BF16)

            scores = None
            for lvl in range(levels, 1, -1):
                bs = 1 << lvl
                r = _ref_rows(b, bs, bs // 2 if rev else bs // 2 - 1, sub8)
                x = b - r
                e = jnp.exp(-jnp.abs(x))
                qe = jnp.where(x <= 0.0, e, 1.0)
                ke = jnp.where(x >= 0.0, e, 1.0)
                s_l = _dot_nt((q * qe).astype(BF16), (k * ke).astype(BF16))
                scores = s_l if scores is None else jnp.where(xor < bs, s_l, scores)
            s_l = _dot_nt((q * f).astype(BF16), k.astype(BF16))
            scores = jnp.where(xor < 2, s_l, scores)
            diag = jnp.sum(q * k, axis=1, keepdims=True)
            scores = jnp.where(xor == 0, diag, scores)
            scores = jnp.where(keep, scores, 0.0)

            edge = b[0:1, :] if rev else b[c - 1:c, :]
            qd = (q * jnp.exp(b)).astype(BF16)
            o = _dot(scores.astype(BF16), vb) + _dot_nt(qd, st.astype(BF16))
            kd = (k * jnp.exp(edge - b)).astype(BF16)
            st_new = st * jnp.exp(edge) + _dot_tn(vb, kd)
            if rev:
                acc_ref[rows, :] = acc_ref[rows, :] + o
            else:
                acc_ref[rows, :] = o
            return st_new

        st_fin = lax.fori_loop(0, n, body, st0)
        if write_state:
            st_ref[1 if rev else 0] = st_fin.T

    run(False)
    run(True)
    y = _rms(acc_ref[...], onw_ref[...])
    gate = g_ref[...]
    o_ref[...] = (y * (gate * _sigmoid(gate))).astype(BF16)


def _hgrn(z, lb_f, lb_b, onorm_w, s0, *, seq, write_state):
    n_tok = z.shape[0]
    nseq = n_tok // seq
    cb = COL_HG // HEAD
    hw = HG_WIDTH // HEAD

    def zspec(piece):
        return pl.BlockSpec((seq, HEAD), lambda b, h, p=piece: (b, cb + p * hw + h))

    vec = pl.BlockSpec((1, HEAD), lambda b, h: (0, h))
    in_specs = [zspec(0), zspec(1), zspec(2), zspec(3), zspec(4), vec, vec,
                pl.BlockSpec((1, HEAD), lambda b, h: (0, 0))]
    args = [z, z, z, z, z, lb_f, lb_b, onorm_w]
    if s0 is not None:
        in_specs.append(pl.BlockSpec((None, 2, None, HEAD, HEAD), lambda b, h: (b, 0, h, 0, 0)))
        args.append(s0)
    out_specs = [pl.BlockSpec((seq, HEAD), lambda b, h: (b, h))]
    out_shape = [jax.ShapeDtypeStruct((n_tok, HG_WIDTH), BF16)]
    if write_state:
        out_specs.append(pl.BlockSpec((None, 2, None, HEAD, HEAD), lambda b, h: (b, 0, h, 0, 0)))
        out_shape.append(jax.ShapeDtypeStruct((nseq, 2, HG_HEADS, HEAD, HEAD), F32))
    res = pl.pallas_call(
        functools.partial(_hgrn_kernel, seq=seq, has_s0=s0 is not None, write_state=write_state),
        grid=(nseq, HG_HEADS),
        in_specs=in_specs, out_specs=out_specs, out_shape=out_shape,
        scratch_shapes=[pltpu.VMEM((seq, HEAD), F32)],
        compiler_params=pltpu.CompilerParams(
            dimension_semantics=("arbitrary", "arbitrary"), vmem_limit_bytes=VMEM_LIMIT),
        name="hgrn",
    )(*args)
    return (res[0], res[1]) if write_state else (res[0], None)


def _rope(x, cos, sin_signed, lane):
    swapped = jnp.where(lane % 64 < 32, pltpu.roll(x, 96, 1), pltpu.roll(x, 32, 1))
    return x * cos + swapped * sin_signed


def _attn_kernel(*refs, seq, past, ctx, qblk):
    q_ref, k_ref, v_ref, g_ref, qw_ref, kw_ref = refs[:6]
    pos = 6
    if ctx:
        ck_ref, cv_ref, cos_ref, sin_ref = refs[pos:pos + 4]
        pos += 4
    o_ref = refs[pos]
    pos += 1
    if not ctx:
        nk_ref, nv_ref = refs[pos:pos + 2]
        pos += 2
    ks_ref, vs_ref = refs[pos:pos + 2]

    k = _rms(k_ref[...], kw_ref[...])
    v = v_ref[...]
    if ctx:
        lane = lax.broadcasted_iota(jnp.int32, (seq, HEAD), 1)
        ks_ref[0:past, :] = ck_ref[...].astype(BF16)
        vs_ref[0:past, :] = cv_ref[...].astype(BF16)
        k = _rope(k, cos_ref[...], sin_ref[...], lane)
    else:
        nk_ref[...] = k
        nv_ref[...] = v
    ks_ref[past:past + seq, :] = k.astype(BF16)
    vs_ref[past:past + seq, :] = v.astype(BF16)

    scale = HEAD ** -0.5
    qw = qw_ref[...]
    for hq in range(AT_GROUP):
        cols = slice(hq * HEAD, (hq + 1) * HEAD)
        for qb in range(seq // qblk):
            rows = slice(qb * qblk, (qb + 1) * qblk)
            q = _rms(q_ref[rows, cols], qw)
            if ctx:
                lane_q = lax.broadcasted_iota(jnp.int32, (qblk, HEAD), 1)
                q = _rope(q, cos_ref[rows, :], sin_ref[rows, :], lane_q)
            s = _dot_nt(q.astype(BF16), ks_ref[...]) * scale
            m = jnp.max(s, axis=-1, keepdims=True)
            p = jnp.exp(s - m)
            den = jnp.sum(p, axis=-1, keepdims=True)
            o = _dot(p.astype(BF16), vs_ref[...]) / den
            gt = g_ref[rows, cols]
            o_ref[rows, cols] = (o * (gt * _sigmoid(gt))).astype(BF16)


def _attention(z, q_norm, k_norm, ctx_k, ctx_v, cos, sin_signed, *, seq):
    n_tok = z.shape[0]
    nseq = n_tok // seq
    ctx = ctx_k is not None
    past = ctx_k.shape[1] if ctx else 0
    gw = AT_GROUP * HEAD
    in_specs = [pl.BlockSpec((seq, gw), lambda b, h: (b, COL_ATQ // gw + h)),
                pl.BlockSpec((seq, HEAD), lambda b, h: (b, COL_ATK // HEAD + h)),
                pl.BlockSpec((seq, HEAD), lambda b, h: (b, COL_ATV // HEAD + h)),
                pl.BlockSpec((seq, gw), lambda b, h: (b, COL_ATG // gw + h)),
                pl.BlockSpec((1, HEAD), lambda b, h: (0, 0)),
                pl.BlockSpec((1, HEAD), lambda b, h: (0, 0))]
    args = [z, z, z, z, q_norm, k_norm]
    if ctx:
        in_specs += [pl.BlockSpec((None, past, HEAD), lambda b, h: (b, 0, h)),
                     pl.BlockSpec((None, past, HEAD), lambda b, h: (b, 0, h)),
                     pl.BlockSpec((seq, HEAD), lambda b, h: (0, 0)),
                     pl.BlockSpec((seq, HEAD), lambda b, h: (0, 0))]
        args += [ctx_k, ctx_v, cos, sin_signed]
    out_specs = [pl.BlockSpec((seq, gw), lambda b, h: (b, h))]
    out_shape = [jax.ShapeDtypeStruct((n_tok, AT_KV_HEADS * gw), BF16)]
    if not ctx:
        kv_spec = pl.BlockSpec((None, seq, HEAD), lambda b, h: (b, 0, h))
        out_specs += [kv_spec, kv_spec]
        out_shape += [jax.ShapeDtypeStruct((nseq, seq, AT_KV_HEADS * HEAD), F32)] * 2
    res = pl.pallas_call(
        functools.partial(_attn_kernel, seq=seq, past=past, ctx=ctx, qblk=min(seq, 256)),
        grid=(nseq, AT_KV_HEADS),
        in_specs=in_specs, out_specs=out_specs, out_shape=out_shape,
        scratch_shapes=[pltpu.VMEM((past + seq, HEAD), BF16), pltpu.VMEM((past + seq, HEAD), BF16)],
        compiler_params=pltpu.CompilerParams(
            dimension_semantics=("arbitrary", "arbitrary"), vmem_limit_bytes=VMEM_LIMIT),
        name="attention",
    )(*args)
    return res if not ctx else (res[0], None, None)


def _s5_scan_kernel(*refs, nb, steps, has_s0, write_state):
    u_ref, wb_ref, a_ref, wc_ref = refs[:4]
    pos = 4
    s0_ref = fin_ref = None
    if has_s0:
        s0_ref = refs[pos]
        pos += 1
    y_ref = refs[pos]
    pos += 1
    if write_state:
        fin_ref = refs[pos]
        pos += 1
    carry_ref, s_ref = refs[pos:pos + 2]

    d = pl.program_id(0)
    tc = pl.program_id(2)
    half = S5_GB * S5_STATE

    @pl.when(tc == 0)
    def _():
        if has_s0:
            carry_ref[...] = s0_ref[...]
        else:
            carry_ref[...] = jnp.zeros_like(carry_ref)

    s_ref[...] = _dot(u_ref[...].astype(BF16), wb_ref[...])

    a_re = jnp.broadcast_to(a_ref[:, 0:half], (8, half))
    a_im = jnp.broadcast_to(a_ref[:, half:2 * half], (8, half))
    for bg in range(nb // 8):
        def step(i, carry):
            sr, sm = carry
            t = jnp.where(d == 0, i, steps - 1 - i)
            rows = pl.ds(pl.multiple_of(t * nb + bg * 8, 8), 8)
            nr = a_re * sr - a_im * sm + s_ref[rows, 0:half]
            nm = a_re * sm + a_im * sr + s_ref[rows, half:2 * half]
            s_ref[rows, 0:half] = nr
            s_ref[rows, half:2 * half] = nm
            return nr, nm

        brows = slice(bg * 8, (bg + 1) * 8)
        sr, sm = lax.fori_loop(0, steps, step,
                               (carry_ref[brows, 0:half], carry_ref[brows, half:2 * half]))
        carry_ref[brows, 0:half] = sr
        carry_ref[brows, half:2 * half] = sm

    y_ref[...] = _dot(s_ref[...].astype(BF16), wc_ref[...])
    if write_state:
        @pl.when(tc == pl.num_programs(2) - 1)
        def _():
            fin_ref[...] = carry_ref[...]


def _s5_scan(u_tm, wb, a_bar, wc, s0, *, nb, write_state):
    n_rows = u_tm.shape[0]
    nt = n_rows // S5_ROWS
    steps = S5_ROWS // nb
    uw = S5_GB * S5_GROUP_CH
    sw = 2 * S5_GB * S5_STATE

    def tblk(d, t):
        return jnp.where(d == 0, t, nt - 1 - t)

    in_specs = [pl.BlockSpec((S5_ROWS, uw), lambda d, g, t: (tblk(d, t), g)),
                pl.BlockSpec((None, None, uw, sw), lambda d, g, t: (d, g, 0, 0)),
                pl.BlockSpec((None, None, 1, sw), lambda d, g, t: (d, g, 0, 0)),
                pl.BlockSpec((None, sw, uw), lambda d, g, t: (g, 0, 0))]
    args = [u_tm, wb, a_bar, wc]
    if s0 is not None:
        in_specs.append(pl.BlockSpec((None, None, nb, sw), lambda d, g, t: (d, g, 0, 0)))
        args.append(s0)
    out_specs = [pl.BlockSpec((None, S5_ROWS, uw), lambda d, g, t: (d, tblk(d, t), g))]
    out_shape = [jax.ShapeDtypeStruct((2, n_rows, HG_WIDTH), F32)]
    if write_state:
        out_specs.append(pl.BlockSpec((None, None, nb, sw), lambda d, g, t: (d, g, 0, 0)))
        out_shape.append(jax.ShapeDtypeStruct((2, S5_NBLK, nb, sw), F32))
    res = pl.pallas_call(
        functools.partial(_s5_scan_kernel, nb=nb, steps=steps, has_s0=s0 is not None,
                          write_state=write_state),
        grid=(2, S5_NBLK, nt),
        in_specs=in_specs, out_specs=out_specs, out_shape=out_shape,
        scratch_shapes=[pltpu.VMEM((nb, sw), F32), pltpu.VMEM((S5_ROWS, sw), F32)],
        compiler_params=pltpu.CompilerParams(
            dimension_semantics=("arbitrary", "arbitrary", "arbitrary"),
            vmem_limit_bytes=VMEM_LIMIT),
        name="s5_scan",
    )(*args)
    return (res[0], res[1]) if write_state else (res[0], None)


def _s5_post_kernel(yf_ref, yb_ref, u_ref, g_ref, d_ref, w_ref, o_ref):
    y = yf_ref[...] + yb_ref[...] + d_ref[...] * u_ref[...]
    y = 0.5 * y * (1.0 + jnp.tanh(math.sqrt(2.0 / math.pi) * (y + 0.044715 * (y * y * y))))
    y = y * _sigmoid(_dot(y.astype(BF16), w_ref[...]))
    gate = g_ref[...]
    o_ref[...] = (y * (gate * _sigmoid(gate))).astype(BF16)


def _s5_post(y2, u_tm2, z, d_skip, w_glu, *, seq):
    n_tok = z.shape[0]
    nseq = n_tok // seq
    w = HG_WIDTH
    return pl.pallas_call(
        _s5_post_kernel,
        grid=(nseq,),
        in_specs=[pl.BlockSpec((None, seq, w), lambda b: (0, 0, b)),
                  pl.BlockSpec((None, seq, w), lambda b: (1, 0, b)),
                  pl.BlockSpec((seq, w), lambda b: (0, b)),
                  pl.BlockSpec((seq, w), lambda b: (b, COL_S5G // w)),
                  pl.BlockSpec((1, w), lambda b: (0, 0)),
                  pl.BlockSpec((w, w), lambda b: (0, 0))],
        out_specs=pl.BlockSpec((seq, w), lambda b: (b, 0)),
        out_shape=jax.ShapeDtypeStruct((n_tok, w), BF16),
        compiler_params=pltpu.CompilerParams(
            dimension_semantics=("arbitrary",), vmem_limit_bytes=VMEM_LIMIT),
        name="s5_post",
    )(y2, y2, u_tm2, z, d_skip, w_glu)


def _merge_kernel(ohg_ref, oat_ref, os5_ref, m0_ref, m1_ref, m2_ref, x_ref, mod_ref,
                  wh_ref, wa_ref, ws_ref, wo_ref, xo_ref):
    merged = _sigmoid(m0_ref[...]) * _dot(ohg_ref[...], wh_ref[...])
    merged = merged + _sigmoid(m1_ref[...]) * _dot(oat_ref[...], wa_ref[...])
    merged = merged + _sigmoid(m2_ref[...]) * _dot(os5_ref[...], ws_ref[...])
    out = _dot(merged.astype(BF16), wo_ref[...])
    xo_ref[...] = x_ref[...] + mod_ref[2:3, :] * out


def _merge_out(o_hg, o_at, o_s5, z, x2, mod3, w_hg, w_at, w_s5, w_out, *, mod_row):
    n_tok, d = x2.shape
    w = HG_WIDTH
    tm = TM_OUT
    act = pl.BlockSpec((tm, w), lambda i: (i, 0))

    def wspec(rows):
        return pl.BlockSpec((rows, d), lambda i: (0, 0), pipeline_mode=pl.Buffered(1))

    return pl.pallas_call(
        _merge_kernel,
        grid=(n_tok // tm,),
        in_specs=[act, act, act,
                  pl.BlockSpec((tm, d), lambda i: (i, 0)),
                  pl.BlockSpec((tm, d), lambda i: (i, 1)),
                  pl.BlockSpec((tm, d), lambda i: (i, 2)),
                  pl.BlockSpec((tm, d), lambda i: (i, 0)),
                  pl.BlockSpec((None, 3, d), lambda i: (mod_row(i), 0, 0)),
                  wspec(w), wspec(w), wspec(w), wspec(d)],
        out_specs=pl.BlockSpec((tm, d), lambda i: (i, 0)),
        out_shape=jax.ShapeDtypeStruct((n_tok, d), F32),
        compiler_params=pltpu.CompilerParams(
            dimension_semantics=("arbitrary",), vmem_limit_bytes=VMEM_LIMIT),
        name="merge_out",
    )(o_hg, o_at, o_s5, z, z, z, x2, mod3, w_hg, w_at, w_s5, w_out)


def _final_norm_kernel(x_ref, w_ref, o_ref):
    o_ref[...] = _rms(x_ref[...], w_ref[...])


def _final_norm(x2, w):
    n_tok, d = x2.shape
    tm = 512
    return pl.pallas_call(
        _final_norm_kernel,
        grid=(n_tok // tm,),
        in_specs=[pl.BlockSpec((tm, d), lambda i: (i, 0)), pl.BlockSpec((1, d), lambda i: (0, 0))],
        out_specs=pl.BlockSpec((tm, d), lambda i: (i, 0)),
        out_shape=jax.ShapeDtypeStruct((n_tok, d), F32),
        compiler_params=pltpu.CompilerParams(dimension_semantics=("arbitrary",)),
        name="final_norm",
    )(x2, w)


def _permute_in_cols(w):
    return jnp.concatenate([w[..., 9728:], w[..., :5120], w[..., 7680:9728], w[..., 5120:6144],
                            w[..., 6656:7680], w[..., 6144:6656]], axis=-1)


def _hgrn_lower_bounds(logits):
    p = jax.nn.softmax(logits.astype(F32), axis=1)
    cs = jnp.cumsum(p, axis=1)
    return cs - cs[:, :1]


def _s5_params(a_re, a_im, log_dt, b_re, b_im, c_re, c_im):
    dt = jnp.exp(log_dt)[..., None]
    mag = jnp.exp(a_re * dt)
    abar_re = mag * jnp.cos(a_im * dt)
    abar_im = mag * jnp.sin(a_im * dt)
    den = a_re * a_re + a_im * a_im
    num_re = abar_re - 1.0
    coef_re = ((num_re * a_re + abar_im * a_im) / den)[..., None]
    coef_im = ((abar_im * a_re - num_re * a_im) / den)[..., None]
    bbar_re = coef_re * b_re - coef_im * b_im
    bbar_im = coef_re * b_im + coef_im * b_re
    half = S5_GB * S5_STATE
    eye = jnp.eye(S5_GB, dtype=F32)
    a_bar = jnp.concatenate([abar_re.reshape(2, S5_NBLK, 1, half),
                             abar_im.reshape(2, S5_NBLK, 1, half)], axis=-1)

    def pack_b(bb):
        bb = bb.reshape(2, S5_NBLK, S5_GB, S5_STATE, S5_GROUP_CH).transpose(0, 1, 2, 4, 3)
        return jnp.einsum('dbgcp,gh->dbgchp', bb, eye).reshape(
            2, S5_NBLK, S5_GB * S5_GROUP_CH, half)

    def pack_c(cc):
        cc = cc.reshape(S5_NBLK, S5_GB, S5_GROUP_CH, S5_STATE).transpose(0, 1, 3, 2)
        return jnp.einsum('bgpc,gh->bgphc', cc, eye).reshape(S5_NBLK, half, S5_GB * S5_GROUP_CH)

    wb = jnp.concatenate([pack_b(bbar_re), pack_b(bbar_im)], axis=-1).astype(BF16)
    wc = jnp.concatenate([pack_c(c_re), -pack_c(c_im)], axis=1).astype(BF16)
    return a_bar, wb, wc


def _rope_tables(seq):
    rows = seq // GRID_W
    row = jnp.repeat(jnp.arange(rows, dtype=F32), GRID_W)
    col = jnp.tile(jnp.arange(GRID_W, dtype=F32), rows)
    quarter = HEAD // 4
    inv = ROPE_THETA ** (-jnp.arange(quarter, dtype=F32) / quarter)
    ang_r = row[:, None] * inv
    ang_c = col[:, None] * inv
    cos = jnp.concatenate([jnp.cos(ang_r)] * 2 + [jnp.cos(ang_c)] * 2, axis=-1)
    sin = jnp.concatenate([-jnp.sin(ang_r), jnp.sin(ang_r), -jnp.sin(ang_c), jnp.sin(ang_c)], axis=-1)
    return cos, sin


def _s5_state_in(s_re, s_im):
    def pack(s):
        b = s.shape[0]
        return s.reshape(b, 2, S5_NBLK, S5_GB * S5_STATE).transpose(1, 2, 0, 3)
    return jnp.concatenate([pack(s_re), pack(s_im)], axis=-1)


def _s5_state_out(fin):
    half = S5_GB * S5_STATE
    b = fin.shape[2]

    def unpack(s):
        return s.transpose(2, 0, 1, 3).reshape(b, 2, S5_GROUPS, S5_STATE)
    return unpack(fin[..., :half]), unpack(fin[..., half:])


def _layer(x2, mod3, mod_row_in, mod_row_out, lw, *, seq, hg_s0, s5_s0, ctx_k, ctx_v, rope, want_state):
    n_tok = x2.shape[0]
    nseq = n_tok // seq
    z, u_tm = _in_proj(x2, mod3, lw['norm_w'], lw['w_in'], seq=seq, mod_row=mod_row_in)
    o_hg, hg_state = _hgrn(z, lw['lb_f'], lw['lb_b'], lw['hg_onorm'], hg_s0, seq=seq,
                           write_state=want_state)
    cos, sin = rope if rope is not None else (None, None)
    o_at, new_k, new_v = _attention(z, lw['at_q_norm'], lw['at_k_norm'], ctx_k, ctx_v, cos, sin, seq=seq)
    y, s5_fin = _s5_scan(u_tm.reshape(seq * nseq, HG_WIDTH), lw['s5_wb'], lw['s5_abar'], lw['s5_wc'],
                         s5_s0, nb=nseq, write_state=want_state)
    o_s5 = _s5_post(y.reshape(2, seq, nseq * HG_WIDTH), u_tm, z, lw['s5_d'], lw['s5_w_glu'], seq=seq)
    x_new = _merge_out(o_hg, o_at, o_s5, z, x2, mod3, lw['w_br_hg'], lw['w_br_at'], lw['w_br_s5'],
                       lw['w_out'], mod_row=mod_row_out)
    return x_new, (new_k, new_v, hg_state, s5_fin)


def kernel(x_prompt, x_sample, cache_k, cache_v, state_hgrn, state_s5_re, state_s5_im, c, c_ctx, norm_w, w_mod, b_mod, w_in, hg_lb_logits, hg_onorm, at_q_norm, at_k_norm, s5_a_re, s5_a_im, s5_log_dt, s5_b_re, s5_b_im, s5_c_re, s5_c_im, s5_d, s5_w_glu, w_br_hg, w_br_at, w_br_s5, w_out, final_norm):
    bsz_p, seq_p, d = x_prompt.shape
    bsz_s, seq_s, _ = x_sample.shape
    depth = w_in.shape[0]
    past = cache_k.shape[2]
    assert TM_IN % seq_p == 0 and seq_s == TM_IN and bsz_s == 8 and bsz_p % 8 == 0
    assert seq_p % HG_CHUNK == 0 and (seq_p * bsz_p) % S5_ROWS == 0 and S5_ROWS % bsz_p == 0

    cvec = jnp.zeros((16, d), F32).at[:bsz_s].set(c).at[bsz_s].set(c_ctx)
    mod = _modulation(cvec, w_mod, b_mod).reshape(depth, 16, 3, d)

    w_in_p = _permute_in_cols(w_in).astype(BF16)
    lower = _hgrn_lower_bounds(hg_lb_logits)
    rope = _rope_tables(seq_s)
    w_glu_b = s5_w_glu.astype(BF16)
    w_hg_b = w_br_hg.astype(BF16)
    w_at_b = w_br_at.astype(BF16)
    w_s5_b = w_br_s5.astype(BF16)
    w_out_b = w_out.astype(BF16)

    layers = []
    for l in range(depth):
        a_bar, wb, wc = _s5_params(s5_a_re[l], s5_a_im[l], s5_log_dt[l], s5_b_re[l], s5_b_im[l],
                                   s5_c_re[l], s5_c_im[l])
        layers.append(dict(
            norm_w=norm_w[l][None, :], w_in=w_in_p[l],
            lb_f=lower[0, l][None, :], lb_b=lower[1, l][None, :], hg_onorm=hg_onorm[l][None, :],
            at_q_norm=at_q_norm[l][None, :], at_k_norm=at_k_norm[l][None, :],
            s5_abar=a_bar, s5_wb=wb, s5_wc=wc, s5_d=s5_d[l][None, :], s5_w_glu=w_glu_b[l],
            w_br_hg=w_hg_b[l], w_br_at=w_at_b[l], w_br_s5=w_s5_b[l], w_out=w_out_b[l]))

    xp = x_prompt.reshape(bsz_p * seq_p, d)
    ks, vs, hs, s5s = [], [], [], []
    for l in range(depth):
        xp, (nk, nv, hst, s5f) = _layer(
            xp, mod[l], lambda i: bsz_s, lambda i: bsz_s, layers[l], seq=seq_p,
            hg_s0=None, s5_s0=None, ctx_k=None, ctx_v=None, rope=None, want_state=True)
        ks.append(nk.reshape(bsz_p, seq_p, AT_KV_HEADS, HEAD))
        vs.append(nv.reshape(bsz_p, seq_p, AT_KV_HEADS, HEAD))
        hs.append(hst)
        s5s.append(_s5_state_out(s5f))
    y_prompt = _final_norm(xp, final_norm[None, :]).reshape(bsz_p, seq_p, d)
    new_cache_k = jnp.stack(ks, axis=1)
    new_cache_v = jnp.stack(vs, axis=1)
    new_state_hgrn = jnp.stack(hs, axis=1)
    new_state_s5_re = jnp.stack([s[0] for s in s5s], axis=1)
    new_state_s5_im = jnp.stack([s[1] for s in s5s], axis=1)

    xs = x_sample.reshape(bsz_s * seq_s, d)
    tiles_per_seq_out = seq_s // TM_OUT
    for l in range(depth):
        xs, _ = _layer(
            xs, mod[l], lambda i: i, lambda i: i // tiles_per_seq_out, layers[l], seq=seq_s,
            hg_s0=state_hgrn[:, l], s5_s0=_s5_state_in(state_s5_re[:, l], state_s5_im[:, l]),
            ctx_k=cache_k[:, l].reshape(bsz_s, past, AT_KV_HEADS * HEAD),
            ctx_v=cache_v[:, l].reshape(bsz_s, past, AT_KV_HEADS * HEAD),
            rope=rope, want_state=False)
    y_sample = _final_norm(xs, final_norm[None, :]).reshape(bsz_s, seq_s, d)

    return (y_prompt, y_sample, new_cache_k, new_cache_v, new_state_hgrn, new_state_s5_re,
            new_state_s5_im)
```

```python
import functools
import math

import jax
import jax.numpy as jnp
from jax import lax
from jax.experimental import pallas as pl
from jax.experimental.pallas import tpu as pltpu

F32 = jnp.float32
BF16 = jnp.bfloat16
EPS = 1e-6

D_MODEL = 2048
HG_WIDTH = 1024
HEAD = 128
HG_HEADS = 8
AT_KV_HEADS = 2
AT_GROUP = 4
S5_GROUPS = 64
S5_STATE = 64
S5_GROUP_CH = 16
GRID_W = 64
ROPE_THETA = 10000.0

COL_MERGE = 0
COL_HG = 6144
COL_S5U = 11264
COL_S5G = 12288
COL_ATQ = 13312
COL_ATG = 14336
COL_ATK = 15360
COL_ATV = 15616
IN_COLS = 15872

TM_IN = 1024
TN_IN = 512
TM_OUT = 256
HG_CHUNK = 64
HG_HB = 4
HG_FAST_MIN_LOG_DECAY = -60.0
S5_GB = 16
S5_NBLK = S5_GROUPS // S5_GB
S5_ROWS = 1024
VMEM_LIMIT = 56 * 1024 * 1024


def _sigmoid(x):
    return 1.0 / (1.0 + jnp.exp(-x))


def _dot(a, b):
    return jnp.dot(a, b, preferred_element_type=F32)


def _dot_nt(a, b):
    return lax.dot_general(a, b, (((1,), (1,)), ((), ())), preferred_element_type=F32)


def _dot_tn(a, b):
    return lax.dot_general(a, b, (((0,), (0,)), ((), ())), preferred_element_type=F32)


def _rms(x, w):
    ms = jnp.mean(x * x, axis=-1, keepdims=True)
    return x * lax.rsqrt(ms + EPS) * w


def _mod_kernel(c_ref, w_ref, b_ref, o_ref):
    c = c_ref[...]
    s = (c * _sigmoid(c)).astype(BF16)
    o_ref[...] = _dot(s, w_ref[...].astype(BF16)) + b_ref[...]


def _modulation(cvec, w_mod, b_mod):
    depth, d, n = w_mod.shape
    tn = 1024
    return pl.pallas_call(
        _mod_kernel,
        grid=(depth, n // tn),
        in_specs=[pl.BlockSpec((16, d), lambda l, j: (0, 0)),
                  pl.BlockSpec((None, d, tn), lambda l, j: (l, 0, j)),
                  pl.BlockSpec((None, 1, tn), lambda l, j: (l, 0, j))],
        out_specs=pl.BlockSpec((None, 16, tn), lambda l, j: (l, 0, j)),
        out_shape=jax.ShapeDtypeStruct((depth, 16, n), F32),
        compiler_params=pltpu.CompilerParams(
            dimension_semantics=("arbitrary", "arbitrary"), vmem_limit_bytes=VMEM_LIMIT),
        name="modulation",
    )(cvec, w_mod, b_mod.reshape(depth, 1, n))


def _in_proj_kernel(x_ref, mod_ref, nw_ref, w_ref, z_ref, u_ref, hn_ref, *, seq, spt):
    j = pl.program_id(1)

    @pl.when(j == 0)
    def _():
        h = _rms(x_ref[...], nw_ref[...])
        h = h * (1.0 + mod_ref[1:2, :]) + mod_ref[0:1, :]
        hn_ref[...] = h.astype(BF16)

    z_ref[...] = _dot(hn_ref[...], w_ref[...])

    for jj in range(HG_WIDTH // TN_IN):
        @pl.when(j == COL_S5U // TN_IN + jj)
        def _():
            for k in range(spt):
                u_ref[:, k * HG_WIDTH + jj * TN_IN:k * HG_WIDTH + (jj + 1) * TN_IN] = (
                    z_ref[k * seq:(k + 1) * seq, :].astype(BF16))


def _in_proj(x2, mod3, norm_w, w, *, seq, mod_row):
    n_tok, d = x2.shape
    nseq = n_tok // seq
    spt = TM_IN // seq
    return pl.pallas_call(
        functools.partial(_in_proj_kernel, seq=seq, spt=spt),
        grid=(n_tok // TM_IN, IN_COLS // TN_IN),
        in_specs=[pl.BlockSpec((TM_IN, d), lambda i, j: (i, 0)),
                  pl.BlockSpec((None, 3, d), lambda i, j: (mod_row(i), 0, 0)),
                  pl.BlockSpec((1, d), lambda i, j: (0, 0)),
                  pl.BlockSpec((d, TN_IN), lambda i, j: (0, j))],
        out_specs=[pl.BlockSpec((TM_IN, TN_IN), lambda i, j: (i, j)),
                   pl.BlockSpec((seq, spt * HG_WIDTH), lambda i, j: (0, i))],
        out_shape=[jax.ShapeDtypeStruct((n_tok, IN_COLS), F32),
                   jax.ShapeDtypeStruct((seq, nseq * HG_WIDTH), BF16)],
        scratch_shapes=[pltpu.VMEM((TM_IN, d), BF16)],
        compiler_params=pltpu.CompilerParams(
            dimension_semantics=("arbitrary", "arbitrary"), vmem_limit_bytes=VMEM_LIMIT),
        name="in_proj",
    )(x2, mod3, norm_w, w)


def _ref_rows(b, bs, ridx, sub8):
    c = b.shape[0]
    if bs >= 8:
        b3 = b.reshape(c // bs, bs, HEAD)
        return jnp.broadcast_to(b3[:, ridx:ridx + 1, :], (c // bs, bs, HEAD)).reshape(c, HEAD)
    b3 = b.reshape(c // 8, 8, HEAD)
    r = None
    for blk in range(8 // bs - 1, -1, -1):
        row = jnp.broadcast_to(b3[:, blk * bs + ridx:blk * bs + ridx + 1, :], (c // 8, 8, HEAD))
        r = row if r is None else jnp.where(sub8 < (blk + 1) * bs, row, r)
    return r.reshape(c, HEAD)


def _hgrn_scores_robust(q, k, f, b, rev, keep, xor, sub8):
    c = b.shape[0]
    scores = None
    for lvl in range(int(math.log2(c)), 1, -1):
        bs = 1 << lvl
        r = _ref_rows(b, bs, bs // 2 if rev else bs // 2 - 1, sub8)
        x = b - r
        e = jnp.exp(-jnp.abs(x))
        qe = jnp.where(x <= 0.0, e, 1.0)
        ke = jnp.where(x >= 0.0, e, 1.0)
        s_l = _dot_nt((q * qe).astype(BF16), (k * ke).astype(BF16))
        scores = s_l if scores is None else jnp.where(xor < bs, s_l, scores)
    s_l = _dot_nt((q * f).astype(BF16), k.astype(BF16))
    scores = jnp.where(xor < 2, s_l, scores)
    diag = jnp.sum(q * k, axis=1, keepdims=True)
    scores = jnp.where(xor == 0, diag, scores)
    return jnp.where(keep, scores, 0.0)


def _cumsum3(tri, g):
    w = g.shape[1]
    hi = g.astype(BF16)
    r1 = g - hi.astype(F32)
    mid = r1.astype(BF16)
    lo = (r1 - mid.astype(F32)).astype(BF16)
    r = _dot(tri, jnp.concatenate([hi, mid, lo], axis=1))
    return r[:, 0:w] + r[:, w:2 * w] + r[:, 2 * w:3 * w]


def _hgrn_kernel(*refs, seq, has_s0, write_state):
    q_ref, v_ref, ff_ref, fb_ref, g_ref, lbf_ref, lbb_ref, onw_ref = refs[:8]
    pos = 8
    s0_ref = st_ref = None
    if has_s0:
        s0_ref = refs[pos]
        pos += 1
    o_ref = refs[pos]
    pos += 1
    if write_state:
        st_ref = refs[pos]
        pos += 1
    acc_ref, lg_ref, k_ref, st_scr, low_ref = refs[pos:pos + 5]

    c = HG_CHUNK
    n = seq // c
    ti = lax.broadcasted_iota(jnp.int32, (c, c), 0)
    si = lax.broadcasted_iota(jnp.int32, (c, c), 1)
    xor = jnp.bitwise_xor(ti, si)
    sub8 = lax.broadcasted_iota(jnp.int32, (c // 8, 8, HEAD), 1)
    keeps = (si <= ti, si >= ti)
    tris = tuple(jnp.where(kp, 1.0, 0.0).astype(BF16) for kp in keeps)
    heads = [slice(h * HEAD, (h + 1) * HEAD) for h in range(HG_HB)]

    for d, fr_ref, lb_ref in ((0, ff_ref, lbf_ref), (1, fb_ref, lbb_ref)):
        lb = lb_ref[...]
        f = lb + (1.0 - lb) * _sigmoid(fr_ref[...])
        lg_ref[d] = jnp.log(f)
        k_ref[d] = 1.0 - f
        for i in range(n):
            low_ref[d, i] = jnp.min(jnp.sum(lg_ref[d, i * c:(i + 1) * c, :], axis=0, keepdims=True))
        for h in range(HG_HB):
            if has_s0:
                st_scr[d, h] = s0_ref[d, h].T
            else:
                st_scr[d, h] = jnp.zeros((HEAD, HEAD), F32)

    def body(ci, carry):
        rows = (pl.ds(pl.multiple_of(ci * c, c), c), pl.ds(pl.multiple_of((n - 1 - ci) * c, c), c))

        def run(fast):
            def fn():
                q = [q_ref[rows[d], :] for d in range(2)]
                k = [k_ref[d, rows[d], :] for d in range(2)]
                lg = [lg_ref[d, rows[d], :] for d in range(2)]
                vb = [v_ref[rows[d], :].astype(BF16) for d in range(2)]
                b = [_cumsum3(tris[d], lg[d]) for d in range(2)]
                edge = [b[0][c - 1:c, :], b[1][0:1, :]]
                qd = [(q[d] * jnp.exp(b[d])).astype(BF16) for d in range(2)]
                scores = [[None] * HG_HB for _ in range(2)]
                if fast:
                    kinv = [k[d] * jnp.exp(-b[d]) for d in range(2)]
                    kb = [kinv[d].astype(BF16) for d in range(2)]
                    kd = [(kinv[d] * jnp.exp(edge[d])).astype(BF16) for d in range(2)]
                    for d in range(2):
                        for h, hs in enumerate(heads):
                            scores[d][h] = jnp.where(keeps[d], _dot_nt(qd[d][:, hs], kb[d][:, hs]), 0.0)
                else:
                    kd = [(k[d] * jnp.exp(edge[d] - b[d])).astype(BF16) for d in range(2)]
                    for d in range(2):
                        for h, hs in enumerate(heads):
                            scores[d][h] = _hgrn_scores_robust(
                                q[d][:, hs], k[d][:, hs], 1.0 - k[d][:, hs], b[d][:, hs], bool(d),
                                keeps[d], xor, sub8)
                for d in range(2):
                    dec = jnp.exp(edge[d])
                    for h, hs in enumerate(heads):
                        st = st_scr[d, h]
                        o = _dot(scores[d][h].astype(BF16), vb[d][:, hs]) + _dot_nt(
                            qd[d][:, hs], st.astype(BF16))
                        st_new = st * dec[:, hs] + _dot_tn(vb[d][:, hs], kd[d][:, hs])
                        st_scr[d, h] = st_new
                        acc_ref[d, rows[d], hs] = o
            return fn

        low = jnp.minimum(low_ref[0, ci], low_ref[1, n - 1 - ci])
        lax.cond(low >= HG_FAST_MIN_LOG_DECAY, run(True), run(False))
        return carry

    lax.fori_loop(0, n, body, 0)
    if write_state:
        for d in range(2):
            for h in range(HG_HB):
                st_ref[d, h] = st_scr[d, h].T
    onw = onw_ref[...]
    for hs in heads:
        y = _rms(acc_ref[0, :, hs] + acc_ref[1, :, hs], onw)
        gate = g_ref[:, hs]
        o_ref[:, hs] = (y * (gate * _sigmoid(gate))).astype(BF16)


def _hgrn(z, lb_f, lb_b, onorm_w, s0, *, seq, write_state):
    n_tok = z.shape[0]
    nseq = n_tok // seq
    bw = HG_HB * HEAD
    cb = COL_HG // bw
    hw = HG_WIDTH // bw

    def zspec(piece):
        return pl.BlockSpec((seq, bw), lambda b, h, p=piece: (b, cb + p * hw + h))

    vec = pl.BlockSpec((1, bw), lambda b, h: (0, h))
    st_spec = pl.BlockSpec((None, 2, HG_HB, HEAD, HEAD), lambda b, h: (b, 0, h, 0, 0))
    in_specs = [zspec(0), zspec(1), zspec(2), zspec(3), zspec(4), vec, vec,
                pl.BlockSpec((1, HEAD), lambda b, h: (0, 0))]
    args = [z, z, z, z, z, lb_f, lb_b, onorm_w]
    if s0 is not None:
        in_specs.append(st_spec)
        args.append(s0)
    out_specs = [pl.BlockSpec((seq, bw), lambda b, h: (b, h))]
    out_shape = [jax.ShapeDtypeStruct((n_tok, HG_WIDTH), BF16)]
    if write_state:
        out_specs.append(st_spec)
        out_shape.append(jax.ShapeDtypeStruct((nseq, 2, HG_HEADS, HEAD, HEAD), F32))
    res = pl.pallas_call(
        functools.partial(_hgrn_kernel, seq=seq, has_s0=s0 is not None, write_state=write_state),
        grid=(nseq, HG_HEADS // HG_HB),
        in_specs=in_specs, out_specs=out_specs, out_shape=out_shape,
        scratch_shapes=[pltpu.VMEM((2, seq, bw), F32), pltpu.VMEM((2, seq, bw), F32),
                        pltpu.VMEM((2, seq, bw), F32), pltpu.VMEM((2, HG_HB, HEAD, HEAD), F32),
                        pltpu.SMEM((2, seq // HG_CHUNK), F32)],
        compiler_params=pltpu.CompilerParams(
            dimension_semantics=("arbitrary", "arbitrary"), vmem_limit_bytes=VMEM_LIMIT),
        name="hgrn",
    )(*args)
    return (res[0], res[1]) if write_state else (res[0], None)


def _rope(x, cos, sin_signed, lane):
    swapped = jnp.where(lane % 64 < 32, pltpu.roll(x, 96, 1), pltpu.roll(x, 32, 1))
    return x * cos + swapped * sin_signed


def _attn_kernel(*refs, seq, past, ctx, qblk):
    q_ref, k_ref, v_ref, g_ref, qw_ref, kw_ref = refs[:6]
    pos = 6
    if ctx:
        ck_ref, cv_ref, cos_ref, sin_ref = refs[pos:pos + 4]
        pos += 4
    o_ref = refs[pos]
    pos += 1
    if not ctx:
        nk_ref, nv_ref = refs[pos:pos + 2]
        pos += 2
    ks_ref, vs_ref = refs[pos:pos + 2]

    k = _rms(k_ref[...], kw_ref[...])
    v = v_ref[...]
    if ctx:
        lane = lax.broadcasted_iota(jnp.int32, (seq, HEAD), 1)
        ks_ref[0:past, :] = ck_ref[...].astype(BF16)
        vs_ref[0:past, :] = cv_ref[...].astype(BF16)
        k = _rope(k, cos_ref[...], sin_ref[...], lane)
    else:
        nk_ref[...] = k
        nv_ref[...] = v
    ks_ref[past:past + seq, :] = k.astype(BF16)
    vs_ref[past:past + seq, :] = v.astype(BF16)

    scale = HEAD ** -0.5
    qw = qw_ref[...]
    for hq in range(AT_GROUP):
        cols = slice(hq * HEAD, (hq + 1) * HEAD)
        for qb in range(seq // qblk):
            rows = slice(qb * qblk, (qb + 1) * qblk)
            q = _rms(q_ref[rows, cols], qw)
            if ctx:
                lane_q = lax.broadcasted_iota(jnp.int32, (qblk, HEAD), 1)
                q = _rope(q, cos_ref[rows, :], sin_ref[rows, :], lane_q)
            s = _dot_nt(q.astype(BF16), ks_ref[...]) * scale
            m = jnp.max(s, axis=-1, keepdims=True)
            p = jnp.exp(s - m)
            den = jnp.sum(p, axis=-1, keepdims=True)
            o = _dot(p.astype(BF16), vs_ref[...]) / den
            gt = g_ref[rows, cols]
            o_ref[rows, cols] = (o * (gt * _sigmoid(gt))).astype(BF16)


def _attention(z, q_norm, k_norm, ctx_k, ctx_v, cos, sin_signed, *, seq):
    n_tok = z.shape[0]
    nseq = n_tok // seq
    ctx = ctx_k is not None
    past = ctx_k.shape[1] if ctx else 0
    gw = AT_GROUP * HEAD
    in_specs = [pl.BlockSpec((seq, gw), lambda b, h: (b, COL_ATQ // gw + h)),
                pl.BlockSpec((seq, HEAD), lambda b, h: (b, COL_ATK // HEAD + h)),
                pl.BlockSpec((seq, HEAD), lambda b, h: (b, COL_ATV // HEAD + h)),
                pl.BlockSpec((seq, gw), lambda b, h: (b, COL_ATG // gw + h)),
                pl.BlockSpec((1, HEAD), lambda b, h: (0, 0)),
                pl.BlockSpec((1, HEAD), lambda b, h: (0, 0))]
    args = [z, z, z, z, q_norm, k_norm]
    if ctx:
        in_specs += [pl.BlockSpec((None, past, HEAD), lambda b, h: (b, 0, h)),
                     pl.BlockSpec((None, past, HEAD), lambda b, h: (b, 0, h)),
                     pl.BlockSpec((seq, HEAD), lambda b, h: (0, 0)),
                     pl.BlockSpec((seq, HEAD), lambda b, h: (0, 0))]
        args += [ctx_k, ctx_v, cos, sin_signed]
    out_specs = [pl.BlockSpec((seq, gw), lambda b, h: (b, h))]
    out_shape = [jax.ShapeDtypeStruct((n_tok, AT_KV_HEADS * gw), BF16)]
    if not ctx:
        kv_spec = pl.BlockSpec((None, seq, HEAD), lambda b, h: (b, 0, h))
        out_specs += [kv_spec, kv_spec]
        out_shape += [jax.ShapeDtypeStruct((nseq, seq, AT_KV_HEADS * HEAD), F32)] * 2
    res = pl.pallas_call(
        functools.partial(_attn_kernel, seq=seq, past=past, ctx=ctx, qblk=min(seq, 256)),
        grid=(nseq, AT_KV_HEADS),
        in_specs=in_specs, out_specs=out_specs, out_shape=out_shape,
        scratch_shapes=[pltpu.VMEM((past + seq, HEAD), BF16), pltpu.VMEM((past + seq, HEAD), BF16)],
        compiler_params=pltpu.CompilerParams(
            dimension_semantics=("arbitrary", "arbitrary"), vmem_limit_bytes=VMEM_LIMIT),
        name="attention",
    )(*args)
    return res if not ctx else (res[0], None, None)


def _s5_scan_kernel(*refs, nb, steps, has_s0, write_state):
    u_ref, wb_ref, a_ref, wc_ref = refs[:4]
    pos = 4
    s0_ref = fin_ref = None
    if has_s0:
        s0_ref = refs[pos]
        pos += 1
    y_ref = refs[pos]
    pos += 1
    if write_state:
        fin_ref = refs[pos]
        pos += 1
    carry_ref, s_ref = refs[pos:pos + 2]

    d = pl.program_id(1)
    tc = pl.program_id(2)
    nt = pl.num_programs(2)
    half = S5_GB * S5_STATE

    @pl.when(tc == 0)
    def _():
        if has_s0:
            carry_ref[...] = s0_ref[...]
        else:
            carry_ref[...] = jnp.zeros_like(carry_ref)

    hr = S5_ROWS // 2
    for r0 in (0, hr):
        s_ref[r0:r0 + hr, :] = _dot(u_ref[r0:r0 + hr, :], wb_ref[...])

    a_re = jnp.broadcast_to(a_ref[:, 0:half], (8, half))
    a_im = jnp.broadcast_to(a_ref[:, half:2 * half], (8, half))
    for bg in range(nb // 8):
        def step(i, carry):
            sr, sm = carry
            t = jnp.where(d == 0, i, steps - 1 - i)
            rows = pl.ds(pl.multiple_of(t * nb + bg * 8, 8), 8)
            nr = a_re * sr - a_im * sm + s_ref[rows, 0:half]
            nm = a_re * sm + a_im * sr + s_ref[rows, half:2 * half]
            s_ref[rows, 0:half] = nr
            s_ref[rows, half:2 * half] = nm
            return nr, nm

        brows = slice(bg * 8, (bg + 1) * 8)
        sr, sm = lax.fori_loop(0, steps, step,
                               (carry_ref[brows, 0:half], carry_ref[brows, half:2 * half]), unroll=4)
        carry_ref[brows, 0:half] = sr
        carry_ref[brows, half:2 * half] = sm

    ys = [_dot(s_ref[r0:r0 + hr, :].astype(BF16), wc_ref[...]) for r0 in (0, hr)]

    @pl.when(d == 0)
    def _():
        for i, r0 in enumerate((0, hr)):
            y_ref[pl.ds(pl.multiple_of(tc * S5_ROWS + r0, hr), hr), :] = ys[i]

    @pl.when(d == 1)
    def _():
        for i, r0 in enumerate((0, hr)):
            rows = pl.ds(pl.multiple_of((nt - 1 - tc) * S5_ROWS + r0, hr), hr)
            y_ref[rows, :] = y_ref[rows, :] + ys[i]

    if write_state:
        @pl.when(tc == nt - 1)
        def _():
            fin_ref[...] = carry_ref[...]


def _s5_scan(u_tm, wb, a_bar, wc, s0, *, nb, write_state):
    n_rows = u_tm.shape[0]
    nt = n_rows // S5_ROWS
    steps = S5_ROWS // nb
    uw = S5_GB * S5_GROUP_CH
    sw = 2 * S5_GB * S5_STATE

    def tblk(d, t):
        return jnp.where(d == 0, t, nt - 1 - t)

    in_specs = [pl.BlockSpec((S5_ROWS, uw), lambda g, d, t: (tblk(d, t), g)),
                pl.BlockSpec((None, None, uw, sw), lambda g, d, t: (d, g, 0, 0)),
                pl.BlockSpec((None, None, 1, sw), lambda g, d, t: (d, g, 0, 0)),
                pl.BlockSpec((None, sw, uw), lambda g, d, t: (g, 0, 0))]
    args = [u_tm, wb, a_bar, wc]
    if s0 is not None:
        in_specs.append(pl.BlockSpec((None, None, nb, sw), lambda g, d, t: (d, g, 0, 0)))
        args.append(s0)
    out_specs = [pl.BlockSpec((n_rows, uw), lambda g, d, t: (0, g))]
    out_shape = [jax.ShapeDtypeStruct((n_rows, HG_WIDTH), F32)]
    if write_state:
        out_specs.append(pl.BlockSpec((None, None, nb, sw), lambda g, d, t: (d, g, 0, 0)))
        out_shape.append(jax.ShapeDtypeStruct((2, S5_NBLK, nb, sw), F32))
    res = pl.pallas_call(
        functools.partial(_s5_scan_kernel, nb=nb, steps=steps, has_s0=s0 is not None,
                          write_state=write_state),
        grid=(S5_NBLK, 2, nt),
        in_specs=in_specs, out_specs=out_specs, out_shape=out_shape,
        scratch_shapes=[pltpu.VMEM((nb, sw), F32), pltpu.VMEM((S5_ROWS, sw), F32)],
        compiler_params=pltpu.CompilerParams(
            dimension_semantics=("arbitrary", "arbitrary", "arbitrary"),
            vmem_limit_bytes=VMEM_LIMIT),
        name="s5_scan",
    )(*args)
    return (res[0], res[1]) if write_state else (res[0], None)


def _s5_post_kernel(y_ref, u_ref, g_ref, d_ref, w_ref, o_ref):
    y = y_ref[...] + d_ref[...] * u_ref[...]
    y = 0.5 * y * (1.0 + jnp.tanh(math.sqrt(2.0 / math.pi) * (y + 0.044715 * (y * y * y))))
    y = y * _sigmoid(_dot(y.astype(BF16), w_ref[...]))
    gate = g_ref[...]
    o_ref[...] = (y * (gate * _sigmoid(gate))).astype(BF16)


def _s5_post(y2, z, d_skip, w_glu, *, seq):
    n_tok = z.shape[0]
    nseq = n_tok // seq
    w = HG_WIDTH
    return pl.pallas_call(
        _s5_post_kernel,
        grid=(nseq,),
        in_specs=[pl.BlockSpec((seq, w), lambda b: (0, b)),
                  pl.BlockSpec((seq, w), lambda b: (b, COL_S5U // w)),
                  pl.BlockSpec((seq, w), lambda b: (b, COL_S5G // w)),
                  pl.BlockSpec((1, w), lambda b: (0, 0)),
                  pl.BlockSpec((w, w), lambda b: (0, 0))],
        out_specs=pl.BlockSpec((seq, w), lambda b: (b, 0)),
        out_shape=jax.ShapeDtypeStruct((n_tok, w), BF16),
        compiler_params=pltpu.CompilerParams(
            dimension_semantics=("arbitrary",), vmem_limit_bytes=VMEM_LIMIT),
        name="s5_post",
    )(y2, z, z, d_skip, w_glu)


def _merge_kernel(ohg_ref, oat_ref, os5_ref, m0_ref, m1_ref, m2_ref, x_ref, mod_ref,
                  wh_ref, wa_ref, ws_ref, wo_ref, xo_ref):
    merged = _sigmoid(m0_ref[...]) * _dot(ohg_ref[...], wh_ref[...])
    merged = merged + _sigmoid(m1_ref[...]) * _dot(oat_ref[...], wa_ref[...])
    merged = merged + _sigmoid(m2_ref[...]) * _dot(os5_ref[...], ws_ref[...])
    out = _dot(merged.astype(BF16), wo_ref[...])
    xo_ref[...] = x_ref[...] + mod_ref[2:3, :] * out


def _merge_out(o_hg, o_at, o_s5, z, x2, mod3, w_hg, w_at, w_s5, w_out, *, mod_row):
    n_tok, d = x2.shape
    w = HG_WIDTH
    tm = TM_OUT
    act = pl.BlockSpec((tm, w), lambda i: (i, 0))

    def wspec(rows):
        return pl.BlockSpec((rows, d), lambda i: (0, 0), pipeline_mode=pl.Buffered(1))

    return pl.pallas_call(
        _merge_kernel,
        grid=(n_tok // tm,),
        in_specs=[act, act, act,
                  pl.BlockSpec((tm, d), lambda i: (i, 0)),
                  pl.BlockSpec((tm, d), lambda i: (i, 1)),
                  pl.BlockSpec((tm, d), lambda i: (i, 2)),
                  pl.BlockSpec((tm, d), lambda i: (i, 0)),
                  pl.BlockSpec((None, 3, d), lambda i: (mod_row(i), 0, 0)),
                  wspec(w), wspec(w), wspec(w), wspec(d)],
        out_specs=pl.BlockSpec((tm, d), lambda i: (i, 0)),
        out_shape=jax.ShapeDtypeStruct((n_tok, d), F32),
        compiler_params=pltpu.CompilerParams(
            dimension_semantics=("arbitrary",), vmem_limit_bytes=VMEM_LIMIT),
        name="merge_out",
    )(o_hg, o_at, o_s5, z, z, z, x2, mod3, w_hg, w_at, w_s5, w_out)


def _final_norm_kernel(x_ref, w_ref, o_ref):
    o_ref[...] = _rms(x_ref[...], w_ref[...])


def _final_norm(x2, w):
    n_tok, d = x2.shape
    tm = 512
    return pl.pallas_call(
        _final_norm_kernel,
        grid=(n_tok // tm,),
        in_specs=[pl.BlockSpec((tm, d), lambda i: (i, 0)), pl.BlockSpec((1, d), lambda i: (0, 0))],
        out_specs=pl.BlockSpec((tm, d), lambda i: (i, 0)),
        out_shape=jax.ShapeDtypeStruct((n_tok, d), F32),
        compiler_params=pltpu.CompilerParams(dimension_semantics=("arbitrary",)),
        name="final_norm",
    )(x2, w)


def _permute_in_cols(w):
    return jnp.concatenate([w[..., 9728:], w[..., :5120], w[..., 7680:9728], w[..., 5120:6144],
                            w[..., 6656:7680], w[..., 6144:6656]], axis=-1)


def _hgrn_lower_bounds(logits):
    p = jax.nn.softmax(logits.astype(F32), axis=1)
    cs = jnp.cumsum(p, axis=1)
    return cs - cs[:, :1]


def _s5_params(a_re, a_im, log_dt, b_re, b_im, c_re, c_im):
    dt = jnp.exp(log_dt)[..., None]
    mag = jnp.exp(a_re * dt)
    abar_re = mag * jnp.cos(a_im * dt)
    abar_im = mag * jnp.sin(a_im * dt)
    den = a_re * a_re + a_im * a_im
    num_re = abar_re - 1.0
    coef_re = ((num_re * a_re + abar_im * a_im) / den)[..., None]
    coef_im = ((abar_im * a_re - num_re * a_im) / den)[..., None]
    bbar_re = coef_re * b_re - coef_im * b_im
    bbar_im = coef_re * b_im + coef_im * b_re
    half = S5_GB * S5_STATE
    eye = jnp.eye(S5_GB, dtype=F32)
    a_bar = jnp.concatenate([abar_re.reshape(2, S5_NBLK, 1, half),
                             abar_im.reshape(2, S5_NBLK, 1, half)], axis=-1)

    def pack_b(bb):
        bb = bb.reshape(2, S5_NBLK, S5_GB, S5_STATE, S5_GROUP_CH).transpose(0, 1, 2, 4, 3)
        return jnp.einsum('dbgcp,gh->dbgchp', bb, eye).reshape(
            2, S5_NBLK, S5_GB * S5_GROUP_CH, half)

    def pack_c(cc):
        cc = cc.reshape(S5_NBLK, S5_GB, S5_GROUP_CH, S5_STATE).transpose(0, 1, 3, 2)
        return jnp.einsum('bgpc,gh->bgphc', cc, eye).reshape(S5_NBLK, half, S5_GB * S5_GROUP_CH)

    wb = jnp.concatenate([pack_b(bbar_re), pack_b(bbar_im)], axis=-1).astype(BF16)
    wc = jnp.concatenate([pack_c(c_re), -pack_c(c_im)], axis=1).astype(BF16)
    return a_bar, wb, wc


def _rope_tables(seq):
    rows = seq // GRID_W
    row = jnp.repeat(jnp.arange(rows, dtype=F32), GRID_W)
    col = jnp.tile(jnp.arange(GRID_W, dtype=F32), rows)
    quarter = HEAD // 4
    inv = ROPE_THETA ** (-jnp.arange(quarter, dtype=F32) / quarter)
    ang_r = row[:, None] * inv
    ang_c = col[:, None] * inv
    cos = jnp.concatenate([jnp.cos(ang_r)] * 2 + [jnp.cos(ang_c)] * 2, axis=-1)
    sin = jnp.concatenate([-jnp.sin(ang_r), jnp.sin(ang_r), -jnp.sin(ang_c), jnp.sin(ang_c)], axis=-1)
    return cos, sin


def _s5_state_in(s_re, s_im):
    def pack(s):
        b = s.shape[0]
        return s.reshape(b, 2, S5_NBLK, S5_GB * S5_STATE).transpose(1, 2, 0, 3)
    return jnp.concatenate([pack(s_re), pack(s_im)], axis=-1)


def _s5_state_out(fin):
    half = S5_GB * S5_STATE
    b = fin.shape[2]

    def unpack(s):
        return s.transpose(2, 0, 1, 3).reshape(b, 2, S5_GROUPS, S5_STATE)
    return unpack(fin[..., :half]), unpack(fin[..., half:])


def _layer(x2, mod3, mod_row_in, mod_row_out, lw, *, seq, hg_s0, s5_s0, ctx_k, ctx_v, rope, want_state):
    n_tok = x2.shape[0]
    nseq = n_tok // seq
    z, u_tm = _in_proj(x2, mod3, lw['norm_w'], lw['w_in'], seq=seq, mod_row=mod_row_in)
    o_hg, hg_state = _hgrn(z, lw['lb_f'], lw['lb_b'], lw['hg_onorm'], hg_s0, seq=seq,
                           write_state=want_state)
    cos, sin = rope if rope is not None else (None, None)
    o_at, new_k, new_v = _attention(z, lw['at_q_norm'], lw['at_k_norm'], ctx_k, ctx_v, cos, sin, seq=seq)
    y, s5_fin = _s5_scan(u_tm.reshape(seq * nseq, HG_WIDTH), lw['s5_wb'], lw['s5_abar'], lw['s5_wc'],
                         s5_s0, nb=nseq, write_state=want_state)
    o_s5 = _s5_post(y.reshape(seq, nseq * HG_WIDTH), z, lw['s5_d'], lw['s5_w_glu'], seq=seq)
    x_new = _merge_out(o_hg, o_at, o_s5, z, x2, mod3, lw['w_br_hg'], lw['w_br_at'], lw['w_br_s5'],
                       lw['w_out'], mod_row=mod_row_out)
    return x_new, (new_k, new_v, hg_state, s5_fin)


def kernel(x_prompt, x_sample, cache_k, cache_v, state_hgrn, state_s5_re, state_s5_im, c, c_ctx, norm_w, w_mod, b_mod, w_in, hg_lb_logits, hg_onorm, at_q_norm, at_k_norm, s5_a_re, s5_a_im, s5_log_dt, s5_b_re, s5_b_im, s5_c_re, s5_c_im, s5_d, s5_w_glu, w_br_hg, w_br_at, w_br_s5, w_out, final_norm):
    bsz_p, seq_p, d = x_prompt.shape
    bsz_s, seq_s, _ = x_sample.shape
    depth = w_in.shape[0]
    past = cache_k.shape[2]
    assert TM_IN % seq_p == 0 and seq_s == TM_IN and bsz_s == 8 and bsz_p % 8 == 0
    assert seq_p % HG_CHUNK == 0 and (seq_p * bsz_p) % S5_ROWS == 0 and S5_ROWS % bsz_p == 0

    cvec = jnp.zeros((16, d), F32).at[:bsz_s].set(c).at[bsz_s].set(c_ctx)
    mod = _modulation(cvec, w_mod, b_mod).reshape(depth, 16, 3, d)

    w_in_p = _permute_in_cols(w_in).astype(BF16)
    lower = _hgrn_lower_bounds(hg_lb_logits)
    rope = _rope_tables(seq_s)
    w_glu_b = s5_w_glu.astype(BF16)
    w_hg_b = w_br_hg.astype(BF16)
    w_at_b = w_br_at.astype(BF16)
    w_s5_b = w_br_s5.astype(BF16)
    w_out_b = w_out.astype(BF16)

    layers = []
    for l in range(depth):
        a_bar, wb, wc = _s5_params(s5_a_re[l], s5_a_im[l], s5_log_dt[l], s5_b_re[l], s5_b_im[l],
                                   s5_c_re[l], s5_c_im[l])
        layers.append(dict(
            norm_w=norm_w[l][None, :], w_in=w_in_p[l],
            lb_f=lower[0, l][None, :], lb_b=lower[1, l][None, :], hg_onorm=hg_onorm[l][None, :],
            at_q_norm=at_q_norm[l][None, :], at_k_norm=at_k_norm[l][None, :],
            s5_abar=a_bar, s5_wb=wb, s5_wc=wc, s5_d=s5_d[l][None, :], s5_w_glu=w_glu_b[l],
            w_br_hg=w_hg_b[l], w_br_at=w_at_b[l], w_br_s5=w_s5_b[l], w_out=w_out_b[l]))

    xp = x_prompt.reshape(bsz_p * seq_p, d)
    ks, vs, hs, s5s = [], [], [], []
    for l in range(depth):
        xp, (nk, nv, hst, s5f) = _layer(
            xp, mod[l], lambda i: bsz_s, lambda i: bsz_s, layers[l], seq=seq_p,
            hg_s0=None, s5_s0=None, ctx_k=None, ctx_v=None, rope=None, want_state=True)
        ks.append(nk.reshape(bsz_p, seq_p, AT_KV_HEADS, HEAD))
        vs.append(nv.reshape(bsz_p, seq_p, AT_KV_HEADS, HEAD))
        hs.append(hst)
        s5s.append(_s5_state_out(s5f))
    y_prompt = _final_norm(xp, final_norm[None, :]).reshape(bsz_p, seq_p, d)
    new_cache_k = jnp.stack(ks, axis=1)
    new_cache_v = jnp.stack(vs, axis=1)
    new_state_hgrn = jnp.stack(hs, axis=1)
    new_state_s5_re = jnp.stack([s[0] for s in s5s], axis=1)
    new_state_s5_im = jnp.stack([s[1] for s in s5s], axis=1)

    xs = x_sample.reshape(bsz_s * seq_s, d)
    tiles_per_seq_out = seq_s // TM_OUT
    for l in range(depth):
        xs, _ = _layer(
            xs, mod[l], lambda i: i, lambda i: i // tiles_per_seq_out, layers[l], seq=seq_s,
            hg_s0=state_hgrn[:, l], s5_s0=_s5_state_in(state_s5_re[:, l], state_s5_im[:, l]),
            ctx_k=cache_k[:, l].reshape(bsz_s, past, AT_KV_HEADS * HEAD),
            ctx_v=cache_v[:, l].reshape(bsz_s, past, AT_KV_HEADS * HEAD),
            rope=rope, want_state=False)
    y_sample = _final_norm(xs, final_norm[None, :]).reshape(bsz_s, seq_s, d)

    return (y_prompt, y_sample, new_cache_k, new_cache_v, new_state_hgrn, new_state_s5_re,
            new_state_s5_im)
```

```python
import functools
import math

import jax
import jax.numpy as jnp
from jax import lax
from jax.experimental import pallas as pl
from jax.experimental.pallas import tpu as pltpu

F32 = jnp.float32
BF16 = jnp.bfloat16
EPS = 1e-6

D_MODEL = 2048
HG_WIDTH = 1024
HEAD = 128
HG_HEADS = 8
AT_KV_HEADS = 2
AT_GROUP = 4
S5_GROUPS = 64
S5_STATE = 64
S5_GROUP_CH = 16
GRID_W = 64
ROPE_THETA = 10000.0

COL_MERGE = 0
COL_HG = 6144
COL_S5U = 11264
COL_S5G = 12288
COL_ATQ = 13312
COL_ATG = 14336
COL_ATK = 15360
COL_ATV = 15616
IN_COLS = 15872

TM_IN = 2048
TN_IN = 512
TM_OUT = 256
HG_CHUNK = 64
HG_HB = 4
HG_FAST_MIN_LOG_DECAY = -60.0
S5_GB = 16
S5_NBLK = S5_GROUPS // S5_GB
S5_NB = 8
S5_ROWS = 1024
VMEM_LIMIT = 56 * 1024 * 1024


def _sigmoid(x):
    return 1.0 / (1.0 + jnp.exp(-x))


def _dot(a, b):
    return jnp.dot(a, b, preferred_element_type=F32)


def _dot_nt(a, b):
    return lax.dot_general(a, b, (((1,), (1,)), ((), ())), preferred_element_type=F32)


def _dot_tn(a, b):
    return lax.dot_general(a, b, (((0,), (0,)), ((), ())), preferred_element_type=F32)


def _rms(x, w):
    ms = jnp.mean(x * x, axis=-1, keepdims=True)
    return x * lax.rsqrt(ms + EPS) * w


def _ada_norm(x, nw, mod_ref):
    return _rms(x, nw) * (1.0 + mod_ref[1:2, :]) + mod_ref[0:1, :]


def _mod_kernel(c_ref, w_ref, b_ref, o_ref):
    c = c_ref[...]
    s = (c * _sigmoid(c)).astype(BF16)
    o_ref[...] = _dot(s, w_ref[...].astype(BF16)) + b_ref[...]


def _modulation(cvec, w_mod, b_mod):
    depth, d, n = w_mod.shape
    tn = 1024
    return pl.pallas_call(
        _mod_kernel,
        grid=(depth, n // tn),
        in_specs=[pl.BlockSpec((16, d), lambda l, j: (0, 0)),
                  pl.BlockSpec((None, d, tn), lambda l, j: (l, 0, j)),
                  pl.BlockSpec((None, 1, tn), lambda l, j: (l, 0, j))],
        out_specs=pl.BlockSpec((None, 16, tn), lambda l, j: (l, 0, j)),
        out_shape=jax.ShapeDtypeStruct((depth, 16, n), F32),
        compiler_params=pltpu.CompilerParams(
            dimension_semantics=("arbitrary", "arbitrary"), vmem_limit_bytes=VMEM_LIMIT),
        name="modulation",
    )(cvec, w_mod, b_mod.reshape(depth, 1, n))


def _norm_kernel(x_ref, mod_ref, nw_ref, h_ref):
    h_ref[...] = _ada_norm(x_ref[...], nw_ref[...], mod_ref).astype(BF16)


def _first_norm(x2, mod3, norm_w, *, mod_row):
    n_tok, d = x2.shape
    tm = TM_OUT
    return pl.pallas_call(
        _norm_kernel,
        grid=(n_tok // tm,),
        in_specs=[pl.BlockSpec((tm, d), lambda i: (i, 0)),
                  pl.BlockSpec((None, 3, d), lambda i: (mod_row(i), 0, 0)),
                  pl.BlockSpec((1, d), lambda i: (0, 0))],
        out_specs=pl.BlockSpec((tm, d), lambda i: (i, 0)),
        out_shape=jax.ShapeDtypeStruct((n_tok, d), BF16),
        compiler_params=pltpu.CompilerParams(dimension_semantics=("arbitrary",)),
        name="first_norm",
    )(x2, mod3, norm_w)


def _w_in_tile(j):
    return jnp.where(j < 12, j + 19,
           jnp.where(j < 22, j - 12,
           jnp.where(j < 26, j - 7,
           jnp.where(j < 28, j - 16,
           jnp.where(j < 30, j - 15, 12)))))


def _in_proj_kernel(h_ref, w_ref, z_ref):
    z_ref[...] = _dot(h_ref[...], w_ref[...].astype(BF16))


def _in_proj(hn, w_in, layer):
    n_tok, d = hn.shape
    return pl.pallas_call(
        _in_proj_kernel,
        grid=(n_tok // TM_IN, IN_COLS // TN_IN),
        in_specs=[pl.BlockSpec((TM_IN, d), lambda i, j: (i, 0)),
                  pl.BlockSpec((None, d, TN_IN), lambda i, j: (layer, 0, _w_in_tile(j)))],
        out_specs=pl.BlockSpec((TM_IN, TN_IN), lambda i, j: (i, j)),
        out_shape=jax.ShapeDtypeStruct((n_tok, IN_COLS), F32),
        compiler_params=pltpu.CompilerParams(
            dimension_semantics=("arbitrary", "arbitrary"), vmem_limit_bytes=VMEM_LIMIT),
        name="in_proj",
    )(hn, w_in)


def _ref_rows(b, bs, ridx, sub8):
    c = b.shape[0]
    if bs >= 8:
        b3 = b.reshape(c // bs, bs, HEAD)
        return jnp.broadcast_to(b3[:, ridx:ridx + 1, :], (c // bs, bs, HEAD)).reshape(c, HEAD)
    b3 = b.reshape(c // 8, 8, HEAD)
    r = None
    for blk in range(8 // bs - 1, -1, -1):
        row = jnp.broadcast_to(b3[:, blk * bs + ridx:blk * bs + ridx + 1, :], (c // 8, 8, HEAD))
        r = row if r is None else jnp.where(sub8 < (blk + 1) * bs, row, r)
    return r.reshape(c, HEAD)


def _hgrn_scores_robust(q, k, f, b, rev, keep, xor, sub8):
    c = b.shape[0]
    scores = None
    for lvl in range(int(math.log2(c)), 1, -1):
        bs = 1 << lvl
        r = _ref_rows(b, bs, bs // 2 if rev else bs // 2 - 1, sub8)
        x = b - r
        e = jnp.exp(-jnp.abs(x))
        qe = jnp.where(x <= 0.0, e, 1.0)
        ke = jnp.where(x >= 0.0, e, 1.0)
        s_l = _dot_nt((q * qe).astype(BF16), (k * ke).astype(BF16))
        scores = s_l if scores is None else jnp.where(xor < bs, s_l, scores)
    s_l = _dot_nt((q * f).astype(BF16), k.astype(BF16))
    scores = jnp.where(xor < 2, s_l, scores)
    diag = jnp.sum(q * k, axis=1, keepdims=True)
    scores = jnp.where(xor == 0, diag, scores)
    return jnp.where(keep, scores, 0.0)


def _cumsum3(tri, g):
    w = g.shape[1]
    hi = g.astype(BF16)
    r1 = g - hi.astype(F32)
    mid = r1.astype(BF16)
    lo = (r1 - mid.astype(F32)).astype(BF16)
    r = _dot(tri, jnp.concatenate([hi, mid, lo], axis=1))
    return r[:, 0:w] + r[:, w:2 * w] + r[:, 2 * w:3 * w]


def _hgrn_kernel(*refs, seq, has_s0, write_state):
    q_ref, v_ref, ff_ref, fb_ref, g_ref, lbf_ref, lbb_ref, onw_ref = refs[:8]
    pos = 8
    s0_ref = st_ref = None
    if has_s0:
        s0_ref = refs[pos]
        pos += 1
    acc_ref, lg_ref, k_ref, st_scr, low_ref = refs[-5:]
    if write_state:
        o_ref, st_ref = refs[-7:-5]
    else:
        o_ref = refs[-6]

    c = HG_CHUNK
    n = seq // c
    ti = lax.broadcasted_iota(jnp.int32, (c, c), 0)
    si = lax.broadcasted_iota(jnp.int32, (c, c), 1)
    xor = jnp.bitwise_xor(ti, si)
    sub8 = lax.broadcasted_iota(jnp.int32, (c // 8, 8, HEAD), 1)
    keeps = (si <= ti, si >= ti)
    tris = tuple(jnp.where(kp, 1.0, 0.0).astype(BF16) for kp in keeps)
    heads = [slice(h * HEAD, (h + 1) * HEAD) for h in range(HG_HB)]

    for d, fr_ref, lb_ref in ((0, ff_ref, lbf_ref), (1, fb_ref, lbb_ref)):
        lb = lb_ref[...]
        f = lb + (1.0 - lb) * _sigmoid(fr_ref[...])
        lg_ref[d] = jnp.log(f)
        k_ref[d] = 1.0 - f
        for i in range(n):
            low_ref[d, i] = jnp.min(jnp.sum(lg_ref[d, i * c:(i + 1) * c, :], axis=0, keepdims=True))
        for h in range(HG_HB):
            if has_s0:
                st_scr[d, h] = s0_ref[d, h].T
            else:
                st_scr[d, h] = jnp.zeros((HEAD, HEAD), F32)

    def body(ci, carry):
        rows = (pl.ds(pl.multiple_of(ci * c, c), c), pl.ds(pl.multiple_of((n - 1 - ci) * c, c), c))

        def run(fast):
            def fn():
                q = [q_ref[rows[d], :] for d in range(2)]
                k = [k_ref[d, rows[d], :] for d in range(2)]
                lg = [lg_ref[d, rows[d], :] for d in range(2)]
                vb = [v_ref[rows[d], :].astype(BF16) for d in range(2)]
                b = [_cumsum3(tris[d], lg[d]) for d in range(2)]
                edge = [b[0][c - 1:c, :], b[1][0:1, :]]
                qd = [(q[d] * jnp.exp(b[d])).astype(BF16) for d in range(2)]
                scores = [[None] * HG_HB for _ in range(2)]
                if fast:
                    kinv = [k[d] * jnp.exp(-b[d]) for d in range(2)]
                    kb = [kinv[d].astype(BF16) for d in range(2)]
                    kd = [(kinv[d] * jnp.exp(edge[d])).astype(BF16) for d in range(2)]
                    for d in range(2):
                        for h, hs in enumerate(heads):
                            scores[d][h] = jnp.where(keeps[d], _dot_nt(qd[d][:, hs], kb[d][:, hs]), 0.0)
                else:
                    kd = [(k[d] * jnp.exp(edge[d] - b[d])).astype(BF16) for d in range(2)]
                    for d in range(2):
                        for h, hs in enumerate(heads):
                            scores[d][h] = _hgrn_scores_robust(
                                q[d][:, hs], k[d][:, hs], 1.0 - k[d][:, hs], b[d][:, hs], bool(d),
                                keeps[d], xor, sub8)
                for d in range(2):
                    dec = jnp.exp(edge[d])
                    for h, hs in enumerate(heads):
                        st = st_scr[d, h]
                        o = _dot(scores[d][h].astype(BF16), vb[d][:, hs]) + _dot_nt(
                            qd[d][:, hs], st.astype(BF16))
                        st_new = st * dec[:, hs] + _dot_tn(vb[d][:, hs], kd[d][:, hs])
                        st_scr[d, h] = st_new
                        acc_ref[d, rows[d], hs] = o
            return fn

        low = jnp.minimum(low_ref[0, ci], low_ref[1, n - 1 - ci])
        lax.cond(low >= HG_FAST_MIN_LOG_DECAY, run(True), run(False))
        return carry

    lax.fori_loop(0, n, body, 0)
    if write_state:
        for d in range(2):
            for h in range(HG_HB):
                st_ref[d, h] = st_scr[d, h].T
    onw = onw_ref[...]
    for hs in heads:
        y = _rms(acc_ref[0, :, hs] + acc_ref[1, :, hs], onw)
        gate = g_ref[:, hs]
        o_ref[:, hs] = (y * (gate * _sigmoid(gate))).astype(BF16)


def _hgrn(z, lb_f, lb_b, onorm_w, s0, states, layer, depth, *, seq, write_state):
    n_tok = z.shape[0]
    nseq = n_tok // seq
    bw = HG_HB * HEAD
    cb = COL_HG // bw
    hw = HG_WIDTH // bw

    def zspec(piece):
        return pl.BlockSpec((seq, bw), lambda b, h, p=piece: (b, cb + p * hw + h))

    vec = pl.BlockSpec((1, bw), lambda b, h: (0, h))
    in_specs = [zspec(0), zspec(1), zspec(2), zspec(3), zspec(4), vec, vec,
                pl.BlockSpec((1, HEAD), lambda b, h: (0, 0))]
    args = [z, z, z, z, z, lb_f, lb_b, onorm_w]
    if s0 is not None:
        in_specs.append(pl.BlockSpec((None, None, 2, HG_HB, HEAD, HEAD), lambda b, h: (b, layer, 0, h, 0, 0)))
        args.append(s0)
    out_specs = [pl.BlockSpec((seq, bw), lambda b, h: (b, h))]
    out_shape = [jax.ShapeDtypeStruct((n_tok, HG_WIDTH), BF16)]
    aliases = {}
    if write_state:
        out_specs.append(pl.BlockSpec((None, None, 2, HG_HB, HEAD, HEAD), lambda b, h: (b, layer, 0, h, 0, 0)))
        out_shape.append(jax.ShapeDtypeStruct((nseq, depth, 2, HG_HEADS, HEAD, HEAD), F32))
        if states is not None:
            aliases = {len(args): 1}
            in_specs.append(pl.BlockSpec(memory_space=pl.ANY))
            args.append(states)
    res = pl.pallas_call(
        functools.partial(_hgrn_kernel, seq=seq, has_s0=s0 is not None, write_state=write_state),
        grid=(nseq, HG_HEADS // HG_HB),
        in_specs=in_specs, out_specs=out_specs, out_shape=out_shape,
        input_output_aliases=aliases,
        scratch_shapes=[pltpu.VMEM((2, seq, bw), F32), pltpu.VMEM((2, seq, bw), F32),
                        pltpu.VMEM((2, seq, bw), F32), pltpu.VMEM((2, HG_HB, HEAD, HEAD), F32),
                        pltpu.SMEM((2, seq // HG_CHUNK), F32)],
        compiler_params=pltpu.CompilerParams(
            dimension_semantics=("arbitrary", "arbitrary"), vmem_limit_bytes=VMEM_LIMIT),
        name="hgrn",
    )(*args)
    return (res[0], res[1]) if write_state else (res[0], None)


def _rope(x, cos, sin_signed, lane):
    swapped = jnp.where(lane % 64 < 32, pltpu.roll(x, 96, 1), pltpu.roll(x, 32, 1))
    return x * cos + swapped * sin_signed


def _attn_kernel(*refs, seq, past, ctx, qblk):
    q_ref, k_ref, v_ref, g_ref, qw_ref, kw_ref = refs[:6]
    pos = 6
    if ctx:
        ck_ref, cv_ref, cos_ref, sin_ref = refs[pos:pos + 4]
        o_ref = refs[-3]
    else:
        o_ref, nk_ref, nv_ref = refs[-5:-2]
    ks_ref, vs_ref = refs[-2:]

    k = _rms(k_ref[...], kw_ref[...])
    v = v_ref[...]
    if ctx:
        lane = lax.broadcasted_iota(jnp.int32, (seq, HEAD), 1)
        ks_ref[0:past, :] = ck_ref[...].astype(BF16)
        vs_ref[0:past, :] = cv_ref[...].astype(BF16)
        k = _rope(k, cos_ref[...], sin_ref[...], lane)
    else:
        nk_ref[...] = k
        nv_ref[...] = v
    ks_ref[past:past + seq, :] = k.astype(BF16)
    vs_ref[past:past + seq, :] = v.astype(BF16)

    scale = HEAD ** -0.5
    qw = qw_ref[...]
    for hq in range(AT_GROUP):
        cols = slice(hq * HEAD, (hq + 1) * HEAD)
        for qb in range(seq // qblk):
            rows = slice(qb * qblk, (qb + 1) * qblk)
            q = _rms(q_ref[rows, cols], qw)
            if ctx:
                lane_q = lax.broadcasted_iota(jnp.int32, (qblk, HEAD), 1)
                q = _rope(q, cos_ref[rows, :], sin_ref[rows, :], lane_q)
            s = _dot_nt(q.astype(BF16), ks_ref[...]) * scale
            m = jnp.max(s, axis=-1, keepdims=True)
            p = jnp.exp(s - m)
            den = jnp.sum(p, axis=-1, keepdims=True)
            o = _dot(p.astype(BF16), vs_ref[...]) / den
            gt = g_ref[rows, cols]
            o_ref[rows, cols] = (o * (gt * _sigmoid(gt))).astype(BF16)


def _attention(z, q_norm, k_norm, ctx_k, ctx_v, cos, sin_signed, caches, layer, depth, *, seq):
    n_tok = z.shape[0]
    nseq = n_tok // seq
    ctx = ctx_k is not None
    past = ctx_k.shape[2] if ctx else 0
    gw = AT_GROUP * HEAD
    in_specs = [pl.BlockSpec((seq, gw), lambda b, h: (b, COL_ATQ // gw + h)),
                pl.BlockSpec((seq, HEAD), lambda b, h: (b, COL_ATK // HEAD + h)),
                pl.BlockSpec((seq, HEAD), lambda b, h: (b, COL_ATV // HEAD + h)),
                pl.BlockSpec((seq, gw), lambda b, h: (b, COL_ATG // gw + h)),
                pl.BlockSpec((1, HEAD), lambda b, h: (0, 0)),
                pl.BlockSpec((1, HEAD), lambda b, h: (0, 0))]
    args = [z, z, z, z, q_norm, k_norm]
    aliases = {}
    if ctx:
        kv_in = pl.BlockSpec((None, None, past, HEAD), lambda b, h: (b, layer, 0, h))
        in_specs += [kv_in, kv_in,
                     pl.BlockSpec((seq, HEAD), lambda b, h: (0, 0)),
                     pl.BlockSpec((seq, HEAD), lambda b, h: (0, 0))]
        args += [ctx_k, ctx_v, cos, sin_signed]
    out_specs = [pl.BlockSpec((seq, gw), lambda b, h: (b, h))]
    out_shape = [jax.ShapeDtypeStruct((n_tok, AT_KV_HEADS * gw), BF16)]
    if not ctx:
        kv_spec = pl.BlockSpec((None, None, seq, HEAD), lambda b, h: (b, layer, 0, h))
        out_specs += [kv_spec, kv_spec]
        out_shape += [jax.ShapeDtypeStruct((nseq, depth, seq, AT_KV_HEADS * HEAD), F32)] * 2
        if caches is not None:
            aliases = {len(args): 1, len(args) + 1: 2}
            in_specs += [pl.BlockSpec(memory_space=pl.ANY)] * 2
            args += list(caches)
    res = pl.pallas_call(
        functools.partial(_attn_kernel, seq=seq, past=past, ctx=ctx, qblk=min(seq, 256)),
        grid=(nseq, AT_KV_HEADS),
        in_specs=in_specs, out_specs=out_specs, out_shape=out_shape,
        input_output_aliases=aliases,
        scratch_shapes=[pltpu.VMEM((past + seq, HEAD), BF16), pltpu.VMEM((past + seq, HEAD), BF16)],
        compiler_params=pltpu.CompilerParams(
            dimension_semantics=("arbitrary", "arbitrary"), vmem_limit_bytes=VMEM_LIMIT),
        name="attention",
    )(*args)
    return (res[0], (res[1], res[2])) if not ctx else (res[0], None)


def _s5_scan_kernel(*refs, has_s0, write_state):
    u_ref, wb_ref, a_ref, wc_ref = refs[:4]
    pos = 4
    s0_ref = fin_ref = None
    if has_s0:
        s0_ref = refs[pos]
        pos += 1
    y_ref = refs[pos]
    pos += 1
    if write_state:
        fin_ref = refs[pos]
    carry_ref, s_ref = refs[-2:]

    d = pl.program_id(2)
    tc = pl.program_id(3)
    nt = pl.num_programs(3)
    half = S5_GB * S5_STATE
    uw = S5_GB * S5_GROUP_CH
    steps = S5_ROWS // S5_NB
    hr = S5_ROWS // 2
    hs = steps // 2

    @pl.when(tc == 0)
    def _():
        if has_s0:
            carry_ref[...] = s0_ref[...]
        else:
            carry_ref[...] = jnp.zeros_like(carry_ref)

    u = pltpu.einshape("btc->tbc", u_ref[...]).reshape(S5_ROWS, uw).astype(BF16)
    for r0 in (0, hr):
        s_ref[r0:r0 + hr, :] = _dot(u[r0:r0 + hr, :], wb_ref[...])

    a_re = jnp.broadcast_to(a_ref[:, 0:half], (S5_NB, half))
    a_im = jnp.broadcast_to(a_ref[:, half:2 * half], (S5_NB, half))

    def step(i, carry):
        sr, sm = carry
        t = jnp.where(d == 0, i, steps - 1 - i)
        rows = pl.ds(pl.multiple_of(t * S5_NB, S5_NB), S5_NB)
        nr = a_re * sr - a_im * sm + s_ref[rows, 0:half]
        nm = a_re * sm + a_im * sr + s_ref[rows, half:2 * half]
        s_ref[rows, 0:half] = nr
        s_ref[rows, half:2 * half] = nm
        return nr, nm

    sr, sm = lax.fori_loop(0, steps, step, (carry_ref[:, 0:half], carry_ref[:, half:2 * half]), unroll=4)
    carry_ref[:, 0:half] = sr
    carry_ref[:, half:2 * half] = sm

    ys = [pltpu.einshape("tbc->btc", _dot(s_ref[r0:r0 + hr, :].astype(BF16), wc_ref[...]).reshape(hs, S5_NB, uw))
          for r0 in (0, hr)]

    @pl.when(d == 0)
    def _():
        for i in range(2):
            y_ref[:, pl.ds(pl.multiple_of(tc * steps + i * hs, hs), hs), :] = ys[i]

    @pl.when(d == 1)
    def _():
        for i in range(2):
            ts = pl.ds(pl.multiple_of((nt - 1 - tc) * steps + i * hs, hs), hs)
            y_ref[:, ts, :] = y_ref[:, ts, :] + ys[i]

    if write_state:
        @pl.when(tc == nt - 1)
        def _():
            fin_ref[...] = carry_ref[...]


def _s5_scan(z, wb, a_bar, wc, s0, *, seq, write_state):
    n_tok = z.shape[0]
    nseq = n_tok // seq
    steps = S5_ROWS // S5_NB
    nt = seq // steps
    uw = S5_GB * S5_GROUP_CH
    sw = 2 * S5_GB * S5_STATE

    def tblk(d, t):
        return jnp.where(d == 0, t, nt - 1 - t)

    st_spec = pl.BlockSpec((None, None, S5_NB, sw), lambda g, b, d, t: (d, g, b, 0))
    in_specs = [pl.BlockSpec((S5_NB, steps, uw), lambda g, b, d, t: (b, tblk(d, t), COL_S5U // uw + g)),
                pl.BlockSpec((None, None, uw, sw), lambda g, b, d, t: (d, g, 0, 0)),
                pl.BlockSpec((None, None, 1, sw), lambda g, b, d, t: (d, g, 0, 0)),
                pl.BlockSpec((None, sw, uw), lambda g, b, d, t: (g, 0, 0))]
    args = [z.reshape(nseq, seq, IN_COLS), wb, a_bar, wc]
    if s0 is not None:
        in_specs.append(st_spec)
        args.append(s0)
    out_specs = [pl.BlockSpec((S5_NB, seq, uw), lambda g, b, d, t: (b, 0, g))]
    out_shape = [jax.ShapeDtypeStruct((nseq, seq, HG_WIDTH), F32)]
    if write_state:
        out_specs.append(st_spec)
        out_shape.append(jax.ShapeDtypeStruct((2, S5_NBLK, nseq, sw), F32))
    res = pl.pallas_call(
        functools.partial(_s5_scan_kernel, has_s0=s0 is not None, write_state=write_state),
        grid=(S5_NBLK, nseq // S5_NB, 2, nt),
        in_specs=in_specs, out_specs=out_specs, out_shape=out_shape,
        scratch_shapes=[pltpu.VMEM((S5_NB, sw), F32), pltpu.VMEM((S5_ROWS, sw), F32)],
        compiler_params=pltpu.CompilerParams(
            dimension_semantics=("arbitrary", "arbitrary", "arbitrary", "arbitrary"),
            vmem_limit_bytes=VMEM_LIMIT),
        name="s5_scan",
    )(*args)
    y = res[0].reshape(n_tok, HG_WIDTH)
    return (y, res[1]) if write_state else (y, None)


def _s5_post_kernel(y_ref, u_ref, g_ref, d_ref, w_ref, o_ref):
    y = y_ref[...] + d_ref[...] * u_ref[...]
    y = 0.5 * y * (1.0 + jnp.tanh(math.sqrt(2.0 / math.pi) * (y + 0.044715 * (y * y * y))))
    y = y * _sigmoid(_dot(y.astype(BF16), w_ref[...]))
    gate = g_ref[...]
    o_ref[...] = (y * (gate * _sigmoid(gate))).astype(BF16)


def _s5_post(y, z, d_skip, w_glu, layer):
    n_tok = z.shape[0]
    w = HG_WIDTH
    tm = 1024
    return pl.pallas_call(
        _s5_post_kernel,
        grid=(n_tok // tm,),
        in_specs=[pl.BlockSpec((tm, w), lambda i: (i, 0)),
                  pl.BlockSpec((tm, w), lambda i: (i, COL_S5U // w)),
                  pl.BlockSpec((tm, w), lambda i: (i, COL_S5G // w)),
                  pl.BlockSpec((1, w), lambda i: (0, 0)),
                  pl.BlockSpec((None, w, w), lambda i: (layer, 0, 0))],
        out_specs=pl.BlockSpec((tm, w), lambda i: (i, 0)),
        out_shape=jax.ShapeDtypeStruct((n_tok, w), BF16),
        compiler_params=pltpu.CompilerParams(
            dimension_semantics=("arbitrary",), vmem_limit_bytes=VMEM_LIMIT),
        name="s5_post",
    )(y, z, z, d_skip, w_glu)


def _merge_kernel(ohg_ref, oat_ref, os5_ref, m0_ref, m1_ref, m2_ref, x_ref, mod_ref,
                  wh_ref, wa_ref, ws_ref, wo_ref, nw_ref, *rest, last):
    merged = _sigmoid(m0_ref[...]) * _dot(ohg_ref[...], wh_ref[...])
    merged = merged + _sigmoid(m1_ref[...]) * _dot(oat_ref[...], wa_ref[...])
    merged = merged + _sigmoid(m2_ref[...]) * _dot(os5_ref[...], ws_ref[...])
    out = _dot(merged.astype(BF16), wo_ref[...])
    x_new = x_ref[...] + mod_ref[2:3, :] * out
    if last:
        (y_ref,) = rest
        y_ref[...] = _rms(x_new, nw_ref[...])
    else:
        modn_ref, xo_ref, hn_ref = rest
        xo_ref[...] = x_new
        hn_ref[...] = _ada_norm(x_new, nw_ref[...], modn_ref).astype(BF16)


def _merge_out(o_hg, o_at, o_s5, z, x2, mod3, w_hg, w_at, w_s5, w_out, next_norm_w, mod3_next,
               layer, *, mod_row):
    n_tok, d = x2.shape
    w = HG_WIDTH
    tm = TM_OUT
    last = mod3_next is None
    act = pl.BlockSpec((tm, w), lambda i: (i, 0))
    row = pl.BlockSpec((tm, d), lambda i: (i, 0))
    mod_spec = pl.BlockSpec((None, 3, d), lambda i: (mod_row(i), 0, 0))

    def wspec(rows):
        return pl.BlockSpec((None, rows, d), lambda i: (layer, 0, 0), pipeline_mode=pl.Buffered(1))

    in_specs = [act, act, act,
                pl.BlockSpec((tm, d), lambda i: (i, 0)),
                pl.BlockSpec((tm, d), lambda i: (i, 1)),
                pl.BlockSpec((tm, d), lambda i: (i, 2)),
                row, mod_spec, wspec(w), wspec(w), wspec(w), wspec(d),
                pl.BlockSpec((1, d), lambda i: (0, 0))]
    args = [o_hg, o_at, o_s5, z, z, z, x2, mod3, w_hg, w_at, w_s5, w_out, next_norm_w]
    if last:
        out_specs = row
        out_shape = jax.ShapeDtypeStruct((n_tok, d), F32)
    else:
        in_specs.append(mod_spec)
        args.append(mod3_next)
        out_specs = [row, row]
        out_shape = [jax.ShapeDtypeStruct((n_tok, d), F32), jax.ShapeDtypeStruct((n_tok, d), BF16)]
    return pl.pallas_call(
        functools.partial(_merge_kernel, last=last),
        grid=(n_tok // tm,),
        in_specs=in_specs, out_specs=out_specs, out_shape=out_shape,
        compiler_params=pltpu.CompilerParams(
            dimension_semantics=("arbitrary",), vmem_limit_bytes=VMEM_LIMIT),
        name="merge_out",
    )(*args)


def _hgrn_lower_bounds(logits):
    p = jax.nn.softmax(logits.astype(F32), axis=1)
    cs = jnp.cumsum(p, axis=1)
    return cs - cs[:, :1]


def _s5_params(a_re, a_im, log_dt, b_re, b_im, c_re, c_im):
    dt = jnp.exp(log_dt)[..., None]
    mag = jnp.exp(a_re * dt)
    abar_re = mag * jnp.cos(a_im * dt)
    abar_im = mag * jnp.sin(a_im * dt)
    den = a_re * a_re + a_im * a_im
    num_re = abar_re - 1.0
    coef_re = ((num_re * a_re + abar_im * a_im) / den)[..., None]
    coef_im = ((abar_im * a_re - num_re * a_im) / den)[..., None]
    bbar_re = coef_re * b_re - coef_im * b_im
    bbar_im = coef_re * b_im + coef_im * b_re
    half = S5_GB * S5_STATE
    eye = jnp.eye(S5_GB, dtype=F32)
    a_bar = jnp.concatenate([abar_re.reshape(2, S5_NBLK, 1, half),
                             abar_im.reshape(2, S5_NBLK, 1, half)], axis=-1)

    def pack_b(bb):
        bb = bb.reshape(2, S5_NBLK, S5_GB, S5_STATE, S5_GROUP_CH).transpose(0, 1, 2, 4, 3)
        return jnp.einsum('dbgcp,gh->dbgchp', bb, eye).reshape(
            2, S5_NBLK, S5_GB * S5_GROUP_CH, half)

    def pack_c(cc):
        cc = cc.reshape(S5_NBLK, S5_GB, S5_GROUP_CH, S5_STATE).transpose(0, 1, 3, 2)
        return jnp.einsum('bgpc,gh->bgphc', cc, eye).reshape(S5_NBLK, half, S5_GB * S5_GROUP_CH)

    wb = jnp.concatenate([pack_b(bbar_re), pack_b(bbar_im)], axis=-1).astype(BF16)
    wc = jnp.concatenate([pack_c(c_re), -pack_c(c_im)], axis=1).astype(BF16)
    return a_bar, wb, wc


def _rope_tables(seq):
    rows = seq // GRID_W
    row = jnp.repeat(jnp.arange(rows, dtype=F32), GRID_W)
    col = jnp.tile(jnp.arange(GRID_W, dtype=F32), rows)
    quarter = HEAD // 4
    inv = ROPE_THETA ** (-jnp.arange(quarter, dtype=F32) / quarter)
    ang_r = row[:, None] * inv
    ang_c = col[:, None] * inv
    cos = jnp.concatenate([jnp.cos(ang_r)] * 2 + [jnp.cos(ang_c)] * 2, axis=-1)
    sin = jnp.concatenate([-jnp.sin(ang_r), jnp.sin(ang_r), -jnp.sin(ang_c), jnp.sin(ang_c)], axis=-1)
    return cos, sin


def _s5_state_in(s_re, s_im):
    def pack(s):
        b = s.shape[0]
        return s.reshape(b, 2, S5_NBLK, S5_GB * S5_STATE).transpose(1, 2, 0, 3)
    return jnp.concatenate([pack(s_re), pack(s_im)], axis=-1)


def _s5_state_out(fin):
    half = S5_GB * S5_STATE
    b = fin.shape[2]

    def unpack(s):
        return s.transpose(2, 0, 1, 3).reshape(b, 2, S5_GROUPS, S5_STATE)
    return unpack(fin[..., :half]), unpack(fin[..., half:])


def _trunk(x2, mod, mod_row, p, *, seq, depth, hg_s0, s5_s0, ctx_k, ctx_v, rope, want_state):
    hn = _first_norm(x2, mod[0], p['norm_w'][0], mod_row=mod_row)
    cos, sin = rope if rope is not None else (None, None)
    caches = hg_states = None
    s5_fins = []
    for l in range(depth):
        z = _in_proj(hn, p['w_in'], l)
        o_hg, hg_states = _hgrn(z, p['lb_f'][l], p['lb_b'][l], p['hg_onorm'][l], hg_s0, hg_states, l, depth,
                                seq=seq, write_state=want_state)
        o_at, caches = _attention(z, p['at_q_norm'][l], p['at_k_norm'][l], ctx_k, ctx_v, cos, sin, caches,
                                  l, depth, seq=seq)
        y, s5_fin = _s5_scan(z, p['s5'][l][1], p['s5'][l][0], p['s5'][l][2],
                             None if s5_s0 is None else s5_s0[l], seq=seq, write_state=want_state)
        s5_fins.append(s5_fin)
        o_s5 = _s5_post(y, z, p['s5_d'][l], p['s5_w_glu'], l)
        last = l == depth - 1
        res = _merge_out(o_hg, o_at, o_s5, z, x2, mod[l], p['w_br_hg'], p['w_br_at'], p['w_br_s5'],
                         p['w_out'], p['final_norm'] if last else p['norm_w'][l + 1],
                         None if last else mod[l + 1], l, mod_row=mod_row)
        if last:
            return res, caches, hg_states, s5_fins
        x2, hn = res


def kernel(x_prompt, x_sample, cache_k, cache_v, state_hgrn, state_s5_re, state_s5_im, c, c_ctx, norm_w, w_mod, b_mod, w_in, hg_lb_logits, hg_onorm, at_q_norm, at_k_norm, s5_a_re, s5_a_im, s5_log_dt, s5_b_re, s5_b_im, s5_c_re, s5_c_im, s5_d, s5_w_glu, w_br_hg, w_br_at, w_br_s5, w_out, final_norm):
    bsz_p, seq_p, d = x_prompt.shape
    bsz_s, seq_s, _ = x_sample.shape
    depth = w_in.shape[0]
    past = cache_k.shape[2]
    steps = S5_ROWS // S5_NB
    assert bsz_s == S5_NB and bsz_p % S5_NB == 0 and seq_p % steps == 0 and seq_s % steps == 0
    assert seq_p % HG_CHUNK == 0 and seq_s % HG_CHUNK == 0 and seq_p % TM_OUT == 0 and seq_s % TM_OUT == 0
    assert (bsz_p * seq_p) % TM_IN == 0 and (bsz_s * seq_s) % TM_IN == 0

    cvec = jnp.zeros((16, d), F32).at[:bsz_s].set(c).at[bsz_s].set(c_ctx)
    mod = _modulation(cvec, w_mod, b_mod).reshape(depth, 16, 3, d)

    lower = _hgrn_lower_bounds(hg_lb_logits)
    p = dict(
        norm_w=norm_w[:, None, :], final_norm=final_norm[None, :], w_in=w_in,
        lb_f=lower[0][:, None, :], lb_b=lower[1][:, None, :], hg_onorm=hg_onorm[:, None, :],
        at_q_norm=at_q_norm[:, None, :], at_k_norm=at_k_norm[:, None, :],
        s5=[_s5_params(s5_a_re[l], s5_a_im[l], s5_log_dt[l], s5_b_re[l], s5_b_im[l], s5_c_re[l], s5_c_im[l])
            for l in range(depth)],
        s5_d=s5_d[:, None, :], s5_w_glu=s5_w_glu.astype(BF16),
        w_br_hg=w_br_hg.astype(BF16), w_br_at=w_br_at.astype(BF16), w_br_s5=w_br_s5.astype(BF16),
        w_out=w_out.astype(BF16))

    yp, caches, hg_states, s5_fins = _trunk(
        x_prompt.reshape(bsz_p * seq_p, d), mod, lambda i: bsz_s, p, seq=seq_p, depth=depth,
        hg_s0=None, s5_s0=None, ctx_k=None, ctx_v=None, rope=None, want_state=True)
    y_prompt = yp.reshape(bsz_p, seq_p, d)
    new_cache_k = caches[0].reshape(bsz_p, depth, seq_p, AT_KV_HEADS, HEAD)
    new_cache_v = caches[1].reshape(bsz_p, depth, seq_p, AT_KV_HEADS, HEAD)
    s5s = [_s5_state_out(f) for f in s5_fins]
    new_state_s5_re = jnp.stack([s[0] for s in s5s], axis=1)
    new_state_s5_im = jnp.stack([s[1] for s in s5s], axis=1)

    tiles_per_seq = seq_s // TM_OUT
    ys, _, _, _ = _trunk(
        x_sample.reshape(bsz_s * seq_s, d), mod, lambda i: i // tiles_per_seq, p, seq=seq_s, depth=depth,
        hg_s0=state_hgrn,
        s5_s0=[_s5_state_in(state_s5_re[:, l], state_s5_im[:, l]) for l in range(depth)],
        ctx_k=cache_k.reshape(bsz_s, depth, past, AT_KV_HEADS * HEAD),
        ctx_v=cache_v.reshape(bsz_s, depth, past, AT_KV_HEADS * HEAD),
        rope=_rope_tables(seq_s), want_state=False)
    y_sample = ys.reshape(bsz_s, seq_s, d)

    return (y_prompt, y_sample, new_cache_k, new_cache_v, hg_states, new_state_s5_re, new_state_s5_im)
```

```python
import functools
import math

import jax
import jax.numpy as jnp
from jax import lax
from jax.experimental import pallas as pl
from jax.experimental.pallas import tpu as pltpu

F32 = jnp.float32
BF16 = jnp.bfloat16
EPS = 1e-6

D_MODEL = 2048
HG_WIDTH = 1024
HEAD = 128
HG_HEADS = 8
AT_KV_HEADS = 2
AT_GROUP = 4
S5_GROUPS = 64
S5_STATE = 64
S5_GROUP_CH = 16
GRID_W = 64
ROPE_THETA = 10000.0

COL_MERGE = 0
COL_HG = 6144
COL_S5U = 11264
COL_S5G = 12288
COL_ATQ = 13312
COL_ATG = 14336
COL_ATK = 15360
COL_ATV = 15616
IN_COLS = 15872

TM_IN = 2048
TN_IN = 512
TM_OUT = 256
HG_CHUNK = 64
HG_HB = 4
HG_FAST_MIN_LOG_DECAY = -60.0
S5_GB = 16
S5_NBLK = S5_GROUPS // S5_GB
S5_NB = 8
S5_ROWS = 1024
VMEM_LIMIT = 56 * 1024 * 1024


def _sigmoid(x):
    return 1.0 / (1.0 + jnp.exp(-x))


def _dot(a, b):
    return jnp.dot(a, b, preferred_element_type=F32)


def _dot_nt(a, b):
    return lax.dot_general(a, b, (((1,), (1,)), ((), ())), preferred_element_type=F32)


def _dot_tn(a, b):
    return lax.dot_general(a, b, (((0,), (0,)), ((), ())), preferred_element_type=F32)


def _rms(x, w):
    ms = jnp.mean(x * x, axis=-1, keepdims=True)
    return x * lax.rsqrt(ms + EPS) * w


def _ada_norm(x, nw, mod_ref):
    return _rms(x, nw) * (1.0 + mod_ref[1:2, :]) + mod_ref[0:1, :]


def _mod_kernel(c_ref, w_ref, b_ref, o_ref):
    c = c_ref[...]
    s = (c * _sigmoid(c)).astype(BF16)
    o_ref[...] = _dot(s, w_ref[...].astype(BF16)) + b_ref[...]


def _modulation(cvec, w_mod, b_mod):
    depth, d, n = w_mod.shape
    tn = 1024
    return pl.pallas_call(
        _mod_kernel,
        grid=(depth, n // tn),
        in_specs=[pl.BlockSpec((16, d), lambda l, j: (0, 0)),
                  pl.BlockSpec((None, d, tn), lambda l, j: (l, 0, j)),
                  pl.BlockSpec((None, 1, tn), lambda l, j: (l, 0, j))],
        out_specs=pl.BlockSpec((None, 16, tn), lambda l, j: (l, 0, j)),
        out_shape=jax.ShapeDtypeStruct((depth, 16, n), F32),
        compiler_params=pltpu.CompilerParams(
            dimension_semantics=("arbitrary", "arbitrary"), vmem_limit_bytes=VMEM_LIMIT),
        name="modulation",
    )(cvec, w_mod, b_mod.reshape(depth, 1, n))


def _norm_kernel(x_ref, mod_ref, nw_ref, h_ref):
    h_ref[...] = _ada_norm(x_ref[...], nw_ref[...], mod_ref).astype(BF16)


def _first_norm(x2, mod3, norm_w, *, mod_row):
    n_tok, d = x2.shape
    tm = TM_OUT
    return pl.pallas_call(
        _norm_kernel,
        grid=(n_tok // tm,),
        in_specs=[pl.BlockSpec((tm, d), lambda i: (i, 0)),
                  pl.BlockSpec((None, 3, d), lambda i: (mod_row(i), 0, 0)),
                  pl.BlockSpec((1, d), lambda i: (0, 0))],
        out_specs=pl.BlockSpec((tm, d), lambda i: (i, 0)),
        out_shape=jax.ShapeDtypeStruct((n_tok, d), BF16),
        compiler_params=pltpu.CompilerParams(dimension_semantics=("arbitrary",)),
        name="first_norm",
    )(x2, mod3, norm_w)


def _w_in_tile(j):
    return jnp.where(j < 12, j + 19,
           jnp.where(j < 22, j - 12,
           jnp.where(j < 26, j - 7,
           jnp.where(j < 28, j - 16,
           jnp.where(j < 30, j - 15, 12)))))


TILE_FORGET0 = (COL_HG + 2 * HG_WIDTH) // TN_IN


def _in_tile_kind(j):
    def tiles(col):
        return col // TN_IN, (col + HG_WIDTH) // TN_IN

    def within(lo_hi):
        return (j >= lo_hi[0]) & (j < lo_hi[1])

    silu = within(tiles(COL_HG + 4 * HG_WIDTH)) | within(tiles(COL_S5G))
    forget = (j >= TILE_FORGET0) & (j < TILE_FORGET0 + 2 * HG_WIDTH // TN_IN)
    return jnp.where(forget, 1, jnp.where(silu, 2, 0))


def _in_proj_kernel(h_ref, w_ref, lb_ref, z_ref):
    def branch(kind):
        def fn():
            z = _dot(h_ref[...], w_ref[...].astype(BF16))
            if kind == 1:
                lb = lb_ref[...]
                z = lb + (1.0 - lb) * _sigmoid(z)
            elif kind == 2:
                z = z * _sigmoid(z)
            z_ref[...] = z
        return fn

    lax.switch(_in_tile_kind(pl.program_id(1)), [branch(kind) for kind in range(3)])


def _in_proj(hn, w_in, lower, layer):
    n_tok, d = hn.shape
    n_forget = 2 * HG_WIDTH // TN_IN
    return pl.pallas_call(
        _in_proj_kernel,
        grid=(n_tok // TM_IN, IN_COLS // TN_IN),
        in_specs=[pl.BlockSpec((TM_IN, d), lambda i, j: (i, 0)),
                  pl.BlockSpec((None, d, TN_IN), lambda i, j: (layer, 0, _w_in_tile(j))),
                  pl.BlockSpec((None, 1, TN_IN),
                               lambda i, j: (layer, 0, jnp.clip(j - TILE_FORGET0, 0, n_forget - 1)))],
        out_specs=pl.BlockSpec((TM_IN, TN_IN), lambda i, j: (i, j)),
        out_shape=jax.ShapeDtypeStruct((n_tok, IN_COLS), F32),
        compiler_params=pltpu.CompilerParams(
            dimension_semantics=("arbitrary", "arbitrary"), vmem_limit_bytes=VMEM_LIMIT),
        name="in_proj",
    )(hn, w_in, lower)


def _ref_rows(b, bs, ridx, sub8):
    c = b.shape[0]
    if bs >= 8:
        b3 = b.reshape(c // bs, bs, HEAD)
        return jnp.broadcast_to(b3[:, ridx:ridx + 1, :], (c // bs, bs, HEAD)).reshape(c, HEAD)
    b3 = b.reshape(c // 8, 8, HEAD)
    r = None
    for blk in range(8 // bs - 1, -1, -1):
        row = jnp.broadcast_to(b3[:, blk * bs + ridx:blk * bs + ridx + 1, :], (c // 8, 8, HEAD))
        r = row if r is None else jnp.where(sub8 < (blk + 1) * bs, row, r)
    return r.reshape(c, HEAD)


def _hgrn_scores_robust(q, k, f, b, rev, keep, xor, sub8):
    c = b.shape[0]
    scores = None
    for lvl in range(int(math.log2(c)), 1, -1):
        bs = 1 << lvl
        r = _ref_rows(b, bs, bs // 2 if rev else bs // 2 - 1, sub8)
        x = b - r
        e = jnp.exp(-jnp.abs(x))
        qe = jnp.where(x <= 0.0, e, 1.0)
        ke = jnp.where(x >= 0.0, e, 1.0)
        s_l = _dot_nt((q * qe).astype(BF16), (k * ke).astype(BF16))
        scores = s_l if scores is None else jnp.where(xor < bs, s_l, scores)
    s_l = _dot_nt((q * f).astype(BF16), k.astype(BF16))
    scores = jnp.where(xor < 2, s_l, scores)
    diag = jnp.sum(q * k, axis=1, keepdims=True)
    scores = jnp.where(xor == 0, diag, scores)
    return jnp.where(keep, scores, 0.0)


def _cumsum3(tri, g):
    w = g.shape[1]
    hi = g.astype(BF16)
    r1 = g - hi.astype(F32)
    mid = r1.astype(BF16)
    lo = (r1 - mid.astype(F32)).astype(BF16)
    r = _dot(tri, jnp.concatenate([hi, mid, lo], axis=1))
    return r[:, 0:w] + r[:, w:2 * w] + r[:, 2 * w:3 * w]


def _hgrn_kernel(*refs, seq, has_s0, write_state):
    q_ref, v_ref, ff_ref, fb_ref, g_ref, onw_ref = refs[:6]
    pos = 6
    s0_ref = st_ref = None
    if has_s0:
        s0_ref = refs[pos]
        pos += 1
    acc_ref, lg_ref, k_ref, st_scr, low_ref = refs[-5:]
    if write_state:
        o_ref, st_ref = refs[-7:-5]
    else:
        o_ref = refs[-6]

    c = HG_CHUNK
    n = seq // c
    ti = lax.broadcasted_iota(jnp.int32, (c, c), 0)
    si = lax.broadcasted_iota(jnp.int32, (c, c), 1)
    xor = jnp.bitwise_xor(ti, si)
    sub8 = lax.broadcasted_iota(jnp.int32, (c // 8, 8, HEAD), 1)
    keeps = (si <= ti, si >= ti)
    tris = tuple(jnp.where(kp, 1.0, 0.0).astype(BF16) for kp in keeps)
    heads = [slice(h * HEAD, (h + 1) * HEAD) for h in range(HG_HB)]

    for d, f_ref in ((0, ff_ref), (1, fb_ref)):
        f = f_ref[...]
        lg_ref[d] = jnp.log(f)
        k_ref[d] = 1.0 - f
        for i in range(n):
            low_ref[d, i] = jnp.min(jnp.sum(lg_ref[d, i * c:(i + 1) * c, :], axis=0, keepdims=True))
        for h in range(HG_HB):
            if has_s0:
                st_scr[d, h] = s0_ref[d, h].T
            else:
                st_scr[d, h] = jnp.zeros((HEAD, HEAD), F32)

    def body(ci, carry):
        rows = (pl.ds(pl.multiple_of(ci * c, c), c), pl.ds(pl.multiple_of((n - 1 - ci) * c, c), c))

        def run(fast):
            def fn():
                q = [q_ref[rows[d], :] for d in range(2)]
                k = [k_ref[d, rows[d], :] for d in range(2)]
                lg = [lg_ref[d, rows[d], :] for d in range(2)]
                vb = [v_ref[rows[d], :].astype(BF16) for d in range(2)]
                b = [_cumsum3(tris[d], lg[d]) for d in range(2)]
                edge = [b[0][c - 1:c, :], b[1][0:1, :]]
                qd = [(q[d] * jnp.exp(b[d])).astype(BF16) for d in range(2)]
                scores = [[None] * HG_HB for _ in range(2)]
                if fast:
                    kinv = [k[d] * jnp.exp(-b[d]) for d in range(2)]
                    kb = [kinv[d].astype(BF16) for d in range(2)]
                    kd = [(kinv[d] * jnp.exp(edge[d])).astype(BF16) for d in range(2)]
                    for d in range(2):
                        for h, hs in enumerate(heads):
                            scores[d][h] = jnp.where(keeps[d], _dot_nt(qd[d][:, hs], kb[d][:, hs]), 0.0)
                else:
                    kd = [(k[d] * jnp.exp(edge[d] - b[d])).astype(BF16) for d in range(2)]
                    for d in range(2):
                        for h, hs in enumerate(heads):
                            scores[d][h] = _hgrn_scores_robust(
                                q[d][:, hs], k[d][:, hs], 1.0 - k[d][:, hs], b[d][:, hs], bool(d),
                                keeps[d], xor, sub8)
                for d in range(2):
                    dec = jnp.exp(edge[d])
                    for h, hs in enumerate(heads):
                        st = st_scr[d, h]
                        o = _dot(scores[d][h].astype(BF16), vb[d][:, hs]) + _dot_nt(
                            qd[d][:, hs], st.astype(BF16))
                        st_new = st * dec[:, hs] + _dot_tn(vb[d][:, hs], kd[d][:, hs])
                        st_scr[d, h] = st_new
                        acc_ref[d, rows[d], hs] = o
            return fn

        low = jnp.minimum(low_ref[0, ci], low_ref[1, n - 1 - ci])
        lax.cond(low >= HG_FAST_MIN_LOG_DECAY, run(True), run(False))
        return carry

    lax.fori_loop(0, n, body, 0)
    if write_state:
        for d in range(2):
            for h in range(HG_HB):
                st_ref[d, h] = st_scr[d, h].T
    onw = onw_ref[...]
    for hs in heads:
        y = _rms(acc_ref[0, :, hs] + acc_ref[1, :, hs], onw)
        o_ref[:, hs] = (y * g_ref[:, hs]).astype(BF16)


def _hgrn(z, onorm_w, s0, states, layer, depth, *, seq, write_state):
    n_tok = z.shape[0]
    nseq = n_tok // seq
    bw = HG_HB * HEAD
    cb = COL_HG // bw
    hw = HG_WIDTH // bw

    def zspec(piece):
        return pl.BlockSpec((seq, bw), lambda b, h, p=piece: (b, cb + p * hw + h))

    in_specs = [zspec(0), zspec(1), zspec(2), zspec(3), zspec(4),
                pl.BlockSpec((1, HEAD), lambda b, h: (0, 0))]
    args = [z, z, z, z, z, onorm_w]
    if s0 is not None:
        in_specs.append(pl.BlockSpec((None, None, 2, HG_HB, HEAD, HEAD), lambda b, h: (b, layer, 0, h, 0, 0)))
        args.append(s0)
    out_specs = [pl.BlockSpec((seq, bw), lambda b, h: (b, h))]
    out_shape = [jax.ShapeDtypeStruct((n_tok, HG_WIDTH), BF16)]
    aliases = {}
    if write_state:
        out_specs.append(pl.BlockSpec((None, None, 2, HG_HB, HEAD, HEAD), lambda b, h: (b, layer, 0, h, 0, 0)))
        out_shape.append(jax.ShapeDtypeStruct((nseq, depth, 2, HG_HEADS, HEAD, HEAD), F32))
        if states is not None:
            aliases = {len(args): 1}
            in_specs.append(pl.BlockSpec(memory_space=pl.ANY))
            args.append(states)
    res = pl.pallas_call(
        functools.partial(_hgrn_kernel, seq=seq, has_s0=s0 is not None, write_state=write_state),
        grid=(nseq, HG_HEADS // HG_HB),
        in_specs=in_specs, out_specs=out_specs, out_shape=out_shape,
        input_output_aliases=aliases,
        scratch_shapes=[pltpu.VMEM((2, seq, bw), F32), pltpu.VMEM((2, seq, bw), F32),
                        pltpu.VMEM((2, seq, bw), F32), pltpu.VMEM((2, HG_HB, HEAD, HEAD), F32),
                        pltpu.SMEM((2, seq // HG_CHUNK), F32)],
        compiler_params=pltpu.CompilerParams(
            dimension_semantics=("arbitrary", "arbitrary"), vmem_limit_bytes=VMEM_LIMIT),
        name="hgrn",
    )(*args)
    return (res[0], res[1]) if write_state else (res[0], None)


def _rope(x, cos, sin_signed, lane):
    swapped = jnp.where(lane % 64 < 32, pltpu.roll(x, 96, 1), pltpu.roll(x, 32, 1))
    return x * cos + swapped * sin_signed


def _attn_kernel(*refs, seq, past, ctx, qblk):
    q_ref, k_ref, v_ref, g_ref, qw_ref, kw_ref = refs[:6]
    pos = 6
    if ctx:
        ck_ref, cv_ref, cos_ref, sin_ref = refs[pos:pos + 4]
        o_ref = refs[-3]
    else:
        o_ref, nk_ref, nv_ref = refs[-5:-2]
    ks_ref, vs_ref = refs[-2:]

    k = _rms(k_ref[...], kw_ref[...])
    v = v_ref[...]
    vs_ref[:, HEAD:2 * HEAD] = jnp.ones((past + seq, HEAD), BF16)
    if ctx:
        lane = lax.broadcasted_iota(jnp.int32, (seq, HEAD), 1)
        ks_ref[0:past, :] = ck_ref[...].astype(BF16)
        vs_ref[0:past, 0:HEAD] = cv_ref[...].astype(BF16)
        k = _rope(k, cos_ref[...], sin_ref[...], lane)
    else:
        nk_ref[...] = k
        nv_ref[...] = v
    ks_ref[past:past + seq, :] = k.astype(BF16)
    vs_ref[past:past + seq, 0:HEAD] = v.astype(BF16)

    qscale = HEAD ** -0.5 * math.log2(math.e)
    qw = qw_ref[...]
    for hq in range(AT_GROUP):
        cols = slice(hq * HEAD, (hq + 1) * HEAD)
        for qb in range(seq // qblk):
            rows = slice(qb * qblk, (qb + 1) * qblk)
            q = _rms(q_ref[rows, cols], qw)
            if ctx:
                lane_q = lax.broadcasted_iota(jnp.int32, (qblk, HEAD), 1)
                q = _rope(q, cos_ref[rows, :], sin_ref[rows, :], lane_q)
            s = _dot_nt((q * qscale).astype(BF16), ks_ref[...])
            p = jnp.exp2(s - jnp.max(s, axis=-1, keepdims=True))
            res = _dot(p.astype(BF16), vs_ref[...])
            o = res[:, 0:HEAD] / res[:, HEAD:2 * HEAD]
            gt = g_ref[rows, cols]
            o_ref[rows, cols] = (o * (gt * _sigmoid(gt))).astype(BF16)


def _attention(z, q_norm, k_norm, ctx_k, ctx_v, cos, sin_signed, caches, layer, depth, *, seq):
    n_tok = z.shape[0]
    nseq = n_tok // seq
    ctx = ctx_k is not None
    past = ctx_k.shape[2] if ctx else 0
    gw = AT_GROUP * HEAD
    in_specs = [pl.BlockSpec((seq, gw), lambda b, h: (b, COL_ATQ // gw + h)),
                pl.BlockSpec((seq, HEAD), lambda b, h: (b, COL_ATK // HEAD + h)),
                pl.BlockSpec((seq, HEAD), lambda b, h: (b, COL_ATV // HEAD + h)),
                pl.BlockSpec((seq, gw), lambda b, h: (b, COL_ATG // gw + h)),
                pl.BlockSpec((1, HEAD), lambda b, h: (0, 0)),
                pl.BlockSpec((1, HEAD), lambda b, h: (0, 0))]
    args = [z, z, z, z, q_norm, k_norm]
    aliases = {}
    if ctx:
        kv_in = pl.BlockSpec((None, None, past, HEAD), lambda b, h: (b, layer, 0, h))
        in_specs += [kv_in, kv_in,
                     pl.BlockSpec((seq, HEAD), lambda b, h: (0, 0)),
                     pl.BlockSpec((seq, HEAD), lambda b, h: (0, 0))]
        args += [ctx_k, ctx_v, cos, sin_signed]
    out_specs = [pl.BlockSpec((seq, gw), lambda b, h: (b, h))]
    out_shape = [jax.ShapeDtypeStruct((n_tok, AT_KV_HEADS * gw), BF16)]
    if not ctx:
        kv_spec = pl.BlockSpec((None, None, seq, HEAD), lambda b, h: (b, layer, 0, h))
        out_specs += [kv_spec, kv_spec]
        out_shape += [jax.ShapeDtypeStruct((nseq, depth, seq, AT_KV_HEADS * HEAD), F32)] * 2
        if caches is not None:
            aliases = {len(args): 1, len(args) + 1: 2}
            in_specs += [pl.BlockSpec(memory_space=pl.ANY)] * 2
            args += list(caches)
    res = pl.pallas_call(
        functools.partial(_attn_kernel, seq=seq, past=past, ctx=ctx, qblk=min(seq, 256)),
        grid=(nseq, AT_KV_HEADS),
        in_specs=in_specs, out_specs=out_specs, out_shape=out_shape,
        input_output_aliases=aliases,
        scratch_shapes=[pltpu.VMEM((past + seq, HEAD), BF16), pltpu.VMEM((past + seq, 2 * HEAD), BF16)],
        compiler_params=pltpu.CompilerParams(
            dimension_semantics=("arbitrary", "arbitrary"), vmem_limit_bytes=VMEM_LIMIT),
        name="attention",
    )(*args)
    return (res[0], (res[1], res[2])) if not ctx else (res[0], None)


def _s5_scan_kernel(*refs, has_s0, write_state):
    u_ref, wb_ref, a_ref, wc_ref = refs[:4]
    pos = 4
    s0_ref = fin_ref = None
    if has_s0:
        s0_ref = refs[pos]
        pos += 1
    y_ref = refs[pos]
    pos += 1
    if write_state:
        fin_ref = refs[pos]
    carry_ref, s_ref = refs[-2:]

    d = pl.program_id(2)
    tc = pl.program_id(3)
    nt = pl.num_programs(3)
    half = S5_GB * S5_STATE
    uw = S5_GB * S5_GROUP_CH
    steps = S5_ROWS // S5_NB
    hr = S5_ROWS // 2
    hs = steps // 2

    @pl.when(tc == 0)
    def _():
        if has_s0:
            carry_ref[...] = s0_ref[...]
        else:
            carry_ref[...] = jnp.zeros_like(carry_ref)

    u = pltpu.einshape("btc->tbc", u_ref[...]).reshape(S5_ROWS, uw).astype(BF16)
    for r0 in (0, hr):
        s_ref[r0:r0 + hr, :] = _dot(u[r0:r0 + hr, :], wb_ref[...])

    a_re = jnp.broadcast_to(a_ref[:, 0:half], (S5_NB, half))
    a_im = jnp.broadcast_to(a_ref[:, half:2 * half], (S5_NB, half))

    def step(i, carry):
        sr, sm = carry
        t = jnp.where(d == 0, i, steps - 1 - i)
        rows = pl.ds(pl.multiple_of(t * S5_NB, S5_NB), S5_NB)
        nr = a_re * sr - a_im * sm + s_ref[rows, 0:half]
        nm = a_re * sm + a_im * sr + s_ref[rows, half:2 * half]
        s_ref[rows, 0:half] = nr
        s_ref[rows, half:2 * half] = nm
        return nr, nm

    sr, sm = lax.fori_loop(0, steps, step, (carry_ref[:, 0:half], carry_ref[:, half:2 * half]), unroll=4)
    carry_ref[:, 0:half] = sr
    carry_ref[:, half:2 * half] = sm

    ys = [pltpu.einshape("tbc->btc", _dot(s_ref[r0:r0 + hr, :].astype(BF16), wc_ref[...]).reshape(hs, S5_NB, uw))
          for r0 in (0, hr)]

    @pl.when(d == 0)
    def _():
        for i in range(2):
            y_ref[:, pl.ds(pl.multiple_of(tc * steps + i * hs, hs), hs), :] = ys[i]

    @pl.when(d == 1)
    def _():
        for i in range(2):
            ts = pl.ds(pl.multiple_of((nt - 1 - tc) * steps + i * hs, hs), hs)
            y_ref[:, ts, :] = y_ref[:, ts, :] + ys[i]

    if write_state:
        @pl.when(tc == nt - 1)
        def _():
            fin_ref[...] = carry_ref[...]


def _s5_scan(z, wb, a_bar, wc, s0, *, seq, write_state):
    n_tok = z.shape[0]
    nseq = n_tok // seq
    steps = S5_ROWS // S5_NB
    nt = seq // steps
    uw = S5_GB * S5_GROUP_CH
    sw = 2 * S5_GB * S5_STATE

    def tblk(d, t):
        return jnp.where(d == 0, t, nt - 1 - t)

    st_spec = pl.BlockSpec((None, None, S5_NB, sw), lambda g, b, d, t: (d, g, b, 0))
    in_specs = [pl.BlockSpec((S5_NB, steps, uw), lambda g, b, d, t: (b, tblk(d, t), COL_S5U // uw + g)),
                pl.BlockSpec((None, None, uw, sw), lambda g, b, d, t: (d, g, 0, 0)),
                pl.BlockSpec((None, None, 1, sw), lambda g, b, d, t: (d, g, 0, 0)),
                pl.BlockSpec((None, sw, uw), lambda g, b, d, t: (g, 0, 0))]
    args = [z.reshape(nseq, seq, IN_COLS), wb, a_bar, wc]
    if s0 is not None:
        in_specs.append(st_spec)
        args.append(s0)
    out_specs = [pl.BlockSpec((S5_NB, seq, uw), lambda g, b, d, t: (b, 0, g))]
    out_shape = [jax.ShapeDtypeStruct((nseq, seq, HG_WIDTH), F32)]
    if write_state:
        out_specs.append(st_spec)
        out_shape.append(jax.ShapeDtypeStruct((2, S5_NBLK, nseq, sw), F32))
    res = pl.pallas_call(
        functools.partial(_s5_scan_kernel, has_s0=s0 is not None, write_state=write_state),
        grid=(S5_NBLK, nseq // S5_NB, 2, nt),
        in_specs=in_specs, out_specs=out_specs, out_shape=out_shape,
        scratch_shapes=[pltpu.VMEM((S5_NB, sw), F32), pltpu.VMEM((S5_ROWS, sw), F32)],
        compiler_params=pltpu.CompilerParams(
            dimension_semantics=("arbitrary", "arbitrary", "arbitrary", "arbitrary"),
            vmem_limit_bytes=VMEM_LIMIT),
        name="s5_scan",
    )(*args)
    y = res[0].reshape(n_tok, HG_WIDTH)
    return (y, res[1]) if write_state else (y, None)


def _s5_post_kernel(y_ref, u_ref, g_ref, d_ref, w_ref, o_ref):
    y = y_ref[...] + d_ref[...] * u_ref[...]
    y = 0.5 * y * (1.0 + jnp.tanh(math.sqrt(2.0 / math.pi) * (y + 0.044715 * (y * y * y))))
    y = y * _sigmoid(_dot(y.astype(BF16), w_ref[...]))
    o_ref[...] = (y * g_ref[...]).astype(BF16)


def _s5_post(y, z, d_skip, w_glu, layer):
    n_tok = z.shape[0]
    w = HG_WIDTH
    tm = 1024
    return pl.pallas_call(
        _s5_post_kernel,
        grid=(n_tok // tm,),
        in_specs=[pl.BlockSpec((tm, w), lambda i: (i, 0)),
                  pl.BlockSpec((tm, w), lambda i: (i, COL_S5U // w)),
                  pl.BlockSpec((tm, w), lambda i: (i, COL_S5G // w)),
                  pl.BlockSpec((1, w), lambda i: (0, 0)),
                  pl.BlockSpec((None, w, w), lambda i: (layer, 0, 0))],
        out_specs=pl.BlockSpec((tm, w), lambda i: (i, 0)),
        out_shape=jax.ShapeDtypeStruct((n_tok, w), BF16),
        compiler_params=pltpu.CompilerParams(
            dimension_semantics=("arbitrary",), vmem_limit_bytes=VMEM_LIMIT),
        name="s5_post",
    )(y, z, z, d_skip, w_glu)


def _merge_kernel(ohg_ref, oat_ref, os5_ref, m0_ref, m1_ref, m2_ref, x_ref, mod_ref,
                  wh_ref, wa_ref, ws_ref, wo_ref, nw_ref, *rest, last):
    merged = _sigmoid(m0_ref[...]) * _dot(ohg_ref[...], wh_ref[...])
    merged = merged + _sigmoid(m1_ref[...]) * _dot(oat_ref[...], wa_ref[...])
    merged = merged + _sigmoid(m2_ref[...]) * _dot(os5_ref[...], ws_ref[...])
    out = _dot(merged.astype(BF16), wo_ref[...])
    x_new = x_ref[...] + mod_ref[2:3, :] * out
    if last:
        (y_ref,) = rest
        y_ref[...] = _rms(x_new, nw_ref[...])
    else:
        modn_ref, xo_ref, hn_ref = rest
        xo_ref[...] = x_new
        hn_ref[...] = _ada_norm(x_new, nw_ref[...], modn_ref).astype(BF16)


def _merge_out(o_hg, o_at, o_s5, z, x2, mod3, w_hg, w_at, w_s5, w_out, next_norm_w, mod3_next,
               layer, *, mod_row):
    n_tok, d = x2.shape
    w = HG_WIDTH
    tm = TM_OUT
    last = mod3_next is None
    act = pl.BlockSpec((tm, w), lambda i: (i, 0))
    row = pl.BlockSpec((tm, d), lambda i: (i, 0))
    mod_spec = pl.BlockSpec((None, 3, d), lambda i: (mod_row(i), 0, 0))

    def wspec(rows):
        return pl.BlockSpec((None, rows, d), lambda i: (layer, 0, 0), pipeline_mode=pl.Buffered(1))

    in_specs = [act, act, act,
                pl.BlockSpec((tm, d), lambda i: (i, 0)),
                pl.BlockSpec((tm, d), lambda i: (i, 1)),
                pl.BlockSpec((tm, d), lambda i: (i, 2)),
                row, mod_spec, wspec(w), wspec(w), wspec(w), wspec(d),
                pl.BlockSpec((1, d), lambda i: (0, 0))]
    args = [o_hg, o_at, o_s5, z, z, z, x2, mod3, w_hg, w_at, w_s5, w_out, next_norm_w]
    if last:
        out_specs = row
        out_shape = jax.ShapeDtypeStruct((n_tok, d), F32)
    else:
        in_specs.append(mod_spec)
        args.append(mod3_next)
        out_specs = [row, row]
        out_shape = [jax.ShapeDtypeStruct((n_tok, d), F32), jax.ShapeDtypeStruct((n_tok, d), BF16)]
    return pl.pallas_call(
        functools.partial(_merge_kernel, last=last),
        grid=(n_tok // tm,),
        in_specs=in_specs, out_specs=out_specs, out_shape=out_shape,
        compiler_params=pltpu.CompilerParams(
            dimension_semantics=("arbitrary",), vmem_limit_bytes=VMEM_LIMIT),
        name="merge_out",
    )(*args)


def _hgrn_lower_bounds(logits):
    p = jax.nn.softmax(logits.astype(F32), axis=1)
    cs = jnp.cumsum(p, axis=1)
    return cs - cs[:, :1]


def _s5_params(a_re, a_im, log_dt, b_re, b_im, c_re, c_im):
    dt = jnp.exp(log_dt)[..., None]
    mag = jnp.exp(a_re * dt)
    abar_re = mag * jnp.cos(a_im * dt)
    abar_im = mag * jnp.sin(a_im * dt)
    den = a_re * a_re + a_im * a_im
    num_re = abar_re - 1.0
    coef_re = ((num_re * a_re + abar_im * a_im) / den)[..., None]
    coef_im = ((abar_im * a_re - num_re * a_im) / den)[..., None]
    bbar_re = coef_re * b_re - coef_im * b_im
    bbar_im = coef_re * b_im + coef_im * b_re
    half = S5_GB * S5_STATE
    eye = jnp.eye(S5_GB, dtype=F32)
    a_bar = jnp.concatenate([abar_re.reshape(2, S5_NBLK, 1, half),
                             abar_im.reshape(2, S5_NBLK, 1, half)], axis=-1)

    def pack_b(bb):
        bb = bb.reshape(2, S5_NBLK, S5_GB, S5_STATE, S5_GROUP_CH).transpose(0, 1, 2, 4, 3)
        return jnp.einsum('dbgcp,gh->dbgchp', bb, eye).reshape(
            2, S5_NBLK, S5_GB * S5_GROUP_CH, half)

    def pack_c(cc):
        cc = cc.reshape(S5_NBLK, S5_GB, S5_GROUP_CH, S5_STATE).transpose(0, 1, 3, 2)
        return jnp.einsum('bgpc,gh->bgphc', cc, eye).reshape(S5_NBLK, half, S5_GB * S5_GROUP_CH)

    wb = jnp.concatenate([pack_b(bbar_re), pack_b(bbar_im)], axis=-1).astype(BF16)
    wc = jnp.concatenate([pack_c(c_re), -pack_c(c_im)], axis=1).astype(BF16)
    return a_bar, wb, wc


def _rope_tables(seq):
    rows = seq // GRID_W
    row = jnp.repeat(jnp.arange(rows, dtype=F32), GRID_W)
    col = jnp.tile(jnp.arange(GRID_W, dtype=F32), rows)
    quarter = HEAD // 4
    inv = ROPE_THETA ** (-jnp.arange(quarter, dtype=F32) / quarter)
    ang_r = row[:, None] * inv
    ang_c = col[:, None] * inv
    cos = jnp.concatenate([jnp.cos(ang_r)] * 2 + [jnp.cos(ang_c)] * 2, axis=-1)
    sin = jnp.concatenate([-jnp.sin(ang_r), jnp.sin(ang_r), -jnp.sin(ang_c), jnp.sin(ang_c)], axis=-1)
    return cos, sin


def _s5_state_in(s_re, s_im):
    def pack(s):
        b = s.shape[0]
        return s.reshape(b, 2, S5_NBLK, S5_GB * S5_STATE).transpose(1, 2, 0, 3)
    return jnp.concatenate([pack(s_re), pack(s_im)], axis=-1)


def _s5_state_out(fin):
    half = S5_GB * S5_STATE
    b = fin.shape[2]

    def unpack(s):
        return s.transpose(2, 0, 1, 3).reshape(b, 2, S5_GROUPS, S5_STATE)
    return unpack(fin[..., :half]), unpack(fin[..., half:])


def _trunk(x2, mod, mod_row, p, *, seq, depth, hg_s0, s5_s0, ctx_k, ctx_v, rope, want_state):
    hn = _first_norm(x2, mod[0], p['norm_w'][0], mod_row=mod_row)
    cos, sin = rope if rope is not None else (None, None)
    caches = hg_states = None
    s5_fins = []
    for l in range(depth):
        z = _in_proj(hn, p['w_in'], p['lower'], l)
        o_hg, hg_states = _hgrn(z, p['hg_onorm'][l], hg_s0, hg_states, l, depth,
                                seq=seq, write_state=want_state)
        o_at, caches = _attention(z, p['at_q_norm'][l], p['at_k_norm'][l], ctx_k, ctx_v, cos, sin, caches,
                                  l, depth, seq=seq)
        y, s5_fin = _s5_scan(z, p['s5'][l][1], p['s5'][l][0], p['s5'][l][2],
                             None if s5_s0 is None else s5_s0[l], seq=seq, write_state=want_state)
        s5_fins.append(s5_fin)
        o_s5 = _s5_post(y, z, p['s5_d'][l], p['s5_w_glu'], l)
        last = l == depth - 1
        res = _merge_out(o_hg, o_at, o_s5, z, x2, mod[l], p['w_br_hg'], p['w_br_at'], p['w_br_s5'],
                         p['w_out'], p['final_norm'] if last else p['norm_w'][l + 1],
                         None if last else mod[l + 1], l, mod_row=mod_row)
        if last:
            return res, caches, hg_states, s5_fins
        x2, hn = res


def kernel(x_prompt, x_sample, cache_k, cache_v, state_hgrn, state_s5_re, state_s5_im, c, c_ctx, norm_w, w_mod, b_mod, w_in, hg_lb_logits, hg_onorm, at_q_norm, at_k_norm, s5_a_re, s5_a_im, s5_log_dt, s5_b_re, s5_b_im, s5_c_re, s5_c_im, s5_d, s5_w_glu, w_br_hg, w_br_at, w_br_s5, w_out, final_norm):
    bsz_p, seq_p, d = x_prompt.shape
    bsz_s, seq_s, _ = x_sample.shape
    depth = w_in.shape[0]
    past = cache_k.shape[2]
    steps = S5_ROWS // S5_NB
    assert bsz_s == S5_NB and bsz_p % S5_NB == 0 and seq_p % steps == 0 and seq_s % steps == 0
    assert seq_p % HG_CHUNK == 0 and seq_s % HG_CHUNK == 0 and seq_p % TM_OUT == 0 and seq_s % TM_OUT == 0
    assert (bsz_p * seq_p) % TM_IN == 0 and (bsz_s * seq_s) % TM_IN == 0

    cvec = jnp.zeros((16, d), F32).at[:bsz_s].set(c).at[bsz_s].set(c_ctx)
    mod = _modulation(cvec, w_mod, b_mod).reshape(depth, 16, 3, d)

    lower = _hgrn_lower_bounds(hg_lb_logits)
    p = dict(
        norm_w=norm_w[:, None, :], final_norm=final_norm[None, :], w_in=w_in,
        lower=jnp.concatenate([lower[0], lower[1]], axis=-1)[:, None, :], hg_onorm=hg_onorm[:, None, :],
        at_q_norm=at_q_norm[:, None, :], at_k_norm=at_k_norm[:, None, :],
        s5=[_s5_params(s5_a_re[l], s5_a_im[l], s5_log_dt[l], s5_b_re[l], s5_b_im[l], s5_c_re[l], s5_c_im[l])
            for l in range(depth)],
        s5_d=s5_d[:, None, :], s5_w_glu=s5_w_glu.astype(BF16),
        w_br_hg=w_br_hg.astype(BF16), w_br_at=w_br_at.astype(BF16), w_br_s5=w_br_s5.astype(BF16),
        w_out=w_out.astype(BF16))

    yp, caches, hg_states, s5_fins = _trunk(
        x_prompt.reshape(bsz_p * seq_p, d), mod, lambda i: bsz_s, p, seq=seq_p, depth=depth,
        hg_s0=None, s5_s0=None, ctx_k=None, ctx_v=None, rope=None, want_state=True)
    y_prompt = yp.reshape(bsz_p, seq_p, d)
    new_cache_k = caches[0].reshape(bsz_p, depth, seq_p, AT_KV_HEADS, HEAD)
    new_cache_v = caches[1].reshape(bsz_p, depth, seq_p, AT_KV_HEADS, HEAD)
    s5s = [_s5_state_out(f) for f in s5_fins]
    new_state_s5_re = jnp.stack([s[0] for s in s5s], axis=1)
    new_state_s5_im = jnp.stack([s[1] for s in s5s], axis=1)

    tiles_per_seq = seq_s // TM_OUT
    ys, _, _, _ = _trunk(
        x_sample.reshape(bsz_s * seq_s, d), mod, lambda i: i // tiles_per_seq, p, seq=seq_s, depth=depth,
        hg_s0=state_hgrn,
        s5_s0=[_s5_state_in(state_s5_re[:, l], state_s5_im[:, l]) for l in range(depth)],
        ctx_k=cache_k.reshape(bsz_s, depth, past, AT_KV_HEADS * HEAD),
        ctx_v=cache_v.reshape(bsz_s, depth, past, AT_KV_HEADS * HEAD),
        rope=_rope_tables(seq_s), want_state=False)
    y_sample = ys.reshape(bsz_s, seq_s, d)

    return (y_prompt, y_sample, new_cache_k, new_cache_v, hg_states, new_state_s5_re, new_state_s5_im)
```

```python
import functools
import math

import jax
import jax.numpy as jnp
from jax import lax
from jax.experimental import pallas as pl
from jax.experimental.pallas import tpu as pltpu

F32 = jnp.float32
BF16 = jnp.bfloat16
EPS = 1e-6

D_MODEL = 2048
HG_WIDTH = 1024
HEAD = 128
HG_HEADS = 8
AT_KV_HEADS = 2
AT_GROUP = 4
S5_GROUPS = 64
S5_STATE = 64
S5_GROUP_CH = 16
GRID_W = 64
ROPE_THETA = 10000.0

COL_MERGE = 0
COL_HG = 6144
COL_S5U = 11264
COL_S5G = 12288
COL_ATQ = 13312
COL_ATG = 14336
COL_ATK = 15360
COL_ATV = 15616
IN_COLS = 15872

TM_IN = 2048
TN_IN = 512
TM_OUT = 256
HG_CHUNK = 64
HG_HB = 4
HG_FAST_MIN_LOG_DECAY = -60.0
S5_GB = 16
S5_NBLK = S5_GROUPS // S5_GB
S5_NB = 8
S5_ROWS = 1024
S5_PIECES = 4
VMEM_LIMIT = 56 * 1024 * 1024


def _sigmoid(x):
    return 1.0 / (1.0 + jnp.exp(-x))


def _dot(a, b):
    return jnp.dot(a, b, preferred_element_type=F32)


def _dot_nt(a, b):
    return lax.dot_general(a, b, (((1,), (1,)), ((), ())), preferred_element_type=F32)


def _dot_tn(a, b):
    return lax.dot_general(a, b, (((0,), (0,)), ((), ())), preferred_element_type=F32)


def _rms(x, w):
    ms = jnp.mean(x * x, axis=-1, keepdims=True)
    return x * lax.rsqrt(ms + EPS) * w


def _ada_norm(x, nw, mod_ref):
    return _rms(x, nw) * (1.0 + mod_ref[1:2, :]) + mod_ref[0:1, :]


def _mod_kernel(c_ref, w_ref, b_ref, o_ref):
    c = c_ref[...]
    s = (c * _sigmoid(c)).astype(BF16)
    o_ref[...] = _dot(s, w_ref[...].astype(BF16)) + b_ref[...]


def _modulation(cvec, w_mod, b_mod):
    depth, d, n = w_mod.shape
    tn = 1024
    return pl.pallas_call(
        _mod_kernel,
        grid=(depth, n // tn),
        in_specs=[pl.BlockSpec((16, d), lambda l, j: (0, 0)),
                  pl.BlockSpec((None, d, tn), lambda l, j: (l, 0, j)),
                  pl.BlockSpec((None, 1, tn), lambda l, j: (l, 0, j))],
        out_specs=pl.BlockSpec((None, 16, tn), lambda l, j: (l, 0, j)),
        out_shape=jax.ShapeDtypeStruct((depth, 16, n), F32),
        compiler_params=pltpu.CompilerParams(
            dimension_semantics=("arbitrary", "arbitrary"), vmem_limit_bytes=VMEM_LIMIT),
        name="modulation",
    )(cvec, w_mod, b_mod.reshape(depth, 1, n))


def _norm_kernel(x_ref, mod_ref, nw_ref, h_ref):
    h_ref[...] = _ada_norm(x_ref[...], nw_ref[...], mod_ref).astype(BF16)


def _first_norm(x2, mod3, norm_w, *, mod_row):
    n_tok, d = x2.shape
    tm = TM_OUT
    return pl.pallas_call(
        _norm_kernel,
        grid=(n_tok // tm,),
        in_specs=[pl.BlockSpec((tm, d), lambda i: (i, 0)),
                  pl.BlockSpec((None, 3, d), lambda i: (mod_row(i), 0, 0)),
                  pl.BlockSpec((1, d), lambda i: (0, 0))],
        out_specs=pl.BlockSpec((tm, d), lambda i: (i, 0)),
        out_shape=jax.ShapeDtypeStruct((n_tok, d), BF16),
        compiler_params=pltpu.CompilerParams(dimension_semantics=("arbitrary",)),
        name="first_norm",
    )(x2, mod3, norm_w)


def _w_in_tile(j):
    return jnp.where(j < 12, j + 19,
           jnp.where(j < 22, j - 12,
           jnp.where(j < 26, j - 7,
           jnp.where(j < 28, j - 16,
           jnp.where(j < 30, j - 15, 12)))))


TILE_FORGET0 = (COL_HG + 2 * HG_WIDTH) // TN_IN


def _in_tile_kind(j):
    def tiles(col):
        return col // TN_IN, (col + HG_WIDTH) // TN_IN

    def within(lo_hi):
        return (j >= lo_hi[0]) & (j < lo_hi[1])

    silu = within(tiles(COL_HG + 4 * HG_WIDTH)) | within(tiles(COL_S5G))
    forget = (j >= TILE_FORGET0) & (j < TILE_FORGET0 + 2 * HG_WIDTH // TN_IN)
    return jnp.where(forget, 1, jnp.where(silu, 2, 0))


def _in_proj_kernel(h_ref, w_ref, lb_ref, z_ref):
    def branch(kind):
        def fn():
            z = _dot(h_ref[...], w_ref[...].astype(BF16))
            if kind == 1:
                lb = lb_ref[...]
                z = lb + (1.0 - lb) * _sigmoid(z)
            elif kind == 2:
                z = z * _sigmoid(z)
            z_ref[...] = z
        return fn

    lax.switch(_in_tile_kind(pl.program_id(1)), [branch(kind) for kind in range(3)])


def _in_proj(hn, w_in, lower, layer):
    n_tok, d = hn.shape
    n_forget = 2 * HG_WIDTH // TN_IN
    return pl.pallas_call(
        _in_proj_kernel,
        grid=(n_tok // TM_IN, IN_COLS // TN_IN),
        in_specs=[pl.BlockSpec((TM_IN, d), lambda i, j: (i, 0)),
                  pl.BlockSpec((None, d, TN_IN), lambda i, j: (layer, 0, _w_in_tile(j))),
                  pl.BlockSpec((None, 1, TN_IN),
                               lambda i, j: (layer, 0, jnp.clip(j - TILE_FORGET0, 0, n_forget - 1)))],
        out_specs=pl.BlockSpec((TM_IN, TN_IN), lambda i, j: (i, j)),
        out_shape=jax.ShapeDtypeStruct((n_tok, IN_COLS), F32),
        compiler_params=pltpu.CompilerParams(
            dimension_semantics=("arbitrary", "arbitrary"), vmem_limit_bytes=VMEM_LIMIT),
        name="in_proj",
    )(hn, w_in, lower)


def _ref_rows(b, bs, ridx, sub8):
    c = b.shape[0]
    if bs >= 8:
        b3 = b.reshape(c // bs, bs, HEAD)
        return jnp.broadcast_to(b3[:, ridx:ridx + 1, :], (c // bs, bs, HEAD)).reshape(c, HEAD)
    b3 = b.reshape(c // 8, 8, HEAD)
    r = None
    for blk in range(8 // bs - 1, -1, -1):
        row = jnp.broadcast_to(b3[:, blk * bs + ridx:blk * bs + ridx + 1, :], (c // 8, 8, HEAD))
        r = row if r is None else jnp.where(sub8 < (blk + 1) * bs, row, r)
    return r.reshape(c, HEAD)


def _hgrn_scores_robust(q, k, f, b, rev, keep, xor, sub8):
    c = b.shape[0]
    scores = None
    for lvl in range(int(math.log2(c)), 1, -1):
        bs = 1 << lvl
        r = _ref_rows(b, bs, bs // 2 if rev else bs // 2 - 1, sub8)
        x = b - r
        e = jnp.exp(-jnp.abs(x))
        qe = jnp.where(x <= 0.0, e, 1.0)
        ke = jnp.where(x >= 0.0, e, 1.0)
        s_l = _dot_nt((q * qe).astype(BF16), (k * ke).astype(BF16))
        scores = s_l if scores is None else jnp.where(xor < bs, s_l, scores)
    s_l = _dot_nt((q * f).astype(BF16), k.astype(BF16))
    scores = jnp.where(xor < 2, s_l, scores)
    diag = jnp.sum(q * k, axis=1, keepdims=True)
    scores = jnp.where(xor == 0, diag, scores)
    return jnp.where(keep, scores, 0.0)


def _cumsum3(tri, g):
    w = g.shape[1]
    hi = g.astype(BF16)
    r1 = g - hi.astype(F32)
    mid = r1.astype(BF16)
    lo = (r1 - mid.astype(F32)).astype(BF16)
    r = _dot(tri, jnp.concatenate([hi, mid, lo], axis=1))
    return r[:, 0:w] + r[:, w:2 * w] + r[:, 2 * w:3 * w]


def _hgrn_kernel(*refs, seq, has_s0, write_state):
    q_ref, v_ref, ff_ref, fb_ref, g_ref, onw_ref = refs[:6]
    pos = 6
    s0_ref = st_ref = None
    if has_s0:
        s0_ref = refs[pos]
        pos += 1
    acc_ref, lg_ref, k_ref, st_scr, low_ref = refs[-5:]
    if write_state:
        o_ref, st_ref = refs[-7:-5]
    else:
        o_ref = refs[-6]

    c = HG_CHUNK
    n = seq // c
    ti = lax.broadcasted_iota(jnp.int32, (c, c), 0)
    si = lax.broadcasted_iota(jnp.int32, (c, c), 1)
    xor = jnp.bitwise_xor(ti, si)
    sub8 = lax.broadcasted_iota(jnp.int32, (c // 8, 8, HEAD), 1)
    keeps = (si <= ti, si >= ti)
    tris = tuple(jnp.where(kp, 1.0, 0.0).astype(BF16) for kp in keeps)
    heads = [slice(h * HEAD, (h + 1) * HEAD) for h in range(HG_HB)]

    for d, f_ref in ((0, ff_ref), (1, fb_ref)):
        f = f_ref[...]
        lg_ref[d] = jnp.log(f)
        k_ref[d] = 1.0 - f
        for i in range(n):
            low_ref[d, i] = jnp.min(jnp.sum(lg_ref[d, i * c:(i + 1) * c, :], axis=0, keepdims=True))
        for h in range(HG_HB):
            if has_s0:
                st_scr[d, h] = s0_ref[d, h].T
            else:
                st_scr[d, h] = jnp.zeros((HEAD, HEAD), F32)

    def body(ci, carry):
        rows = (pl.ds(pl.multiple_of(ci * c, c), c), pl.ds(pl.multiple_of((n - 1 - ci) * c, c), c))

        def run(fast):
            def fn():
                q = [q_ref[rows[d], :] for d in range(2)]
                k = [k_ref[d, rows[d], :] for d in range(2)]
                lg = [lg_ref[d, rows[d], :] for d in range(2)]
                vb = [v_ref[rows[d], :].astype(BF16) for d in range(2)]
                b = [_cumsum3(tris[d], lg[d]) for d in range(2)]
                edge = [b[0][c - 1:c, :], b[1][0:1, :]]
                qd = [(q[d] * jnp.exp(b[d])).astype(BF16) for d in range(2)]
                scores = [[None] * HG_HB for _ in range(2)]
                if fast:
                    kinv = [k[d] * jnp.exp(-b[d]) for d in range(2)]
                    kb = [kinv[d].astype(BF16) for d in range(2)]
                    kd = [(kinv[d] * jnp.exp(edge[d])).astype(BF16) for d in range(2)]
                    for d in range(2):
                        for h, hs in enumerate(heads):
                            scores[d][h] = jnp.where(keeps[d], _dot_nt(qd[d][:, hs], kb[d][:, hs]), 0.0)
                else:
                    kd = [(k[d] * jnp.exp(edge[d] - b[d])).astype(BF16) for d in range(2)]
                    for d in range(2):
                        for h, hs in enumerate(heads):
                            scores[d][h] = _hgrn_scores_robust(
                                q[d][:, hs], k[d][:, hs], 1.0 - k[d][:, hs], b[d][:, hs], bool(d),
                                keeps[d], xor, sub8)
                for d in range(2):
                    dec = jnp.exp(edge[d])
                    for h, hs in enumerate(heads):
                        st = st_scr[d, h]
                        o = _dot(scores[d][h].astype(BF16), vb[d][:, hs]) + _dot_nt(
                            qd[d][:, hs], st.astype(BF16))
                        st_new = st * dec[:, hs] + _dot_tn(vb[d][:, hs], kd[d][:, hs])
                        st_scr[d, h] = st_new
                        acc_ref[d, rows[d], hs] = o
            return fn

        low = jnp.minimum(low_ref[0, ci], low_ref[1, n - 1 - ci])
        lax.cond(low >= HG_FAST_MIN_LOG_DECAY, run(True), run(False))
        return carry

    lax.fori_loop(0, n, body, 0)
    if write_state:
        for d in range(2):
            for h in range(HG_HB):
                st_ref[d, h] = st_scr[d, h].T
    onw = onw_ref[...]
    for hs in heads:
        y = _rms(acc_ref[0, :, hs] + acc_ref[1, :, hs], onw)
        o_ref[:, hs] = (y * g_ref[:, hs]).astype(BF16)


def _hgrn(z, onorm_w, s0, states, layer, depth, *, seq, write_state):
    n_tok = z.shape[0]
    nseq = n_tok // seq
    bw = HG_HB * HEAD
    cb = COL_HG // bw
    hw = HG_WIDTH // bw

    def zspec(piece):
        return pl.BlockSpec((seq, bw), lambda b, h, p=piece: (b, cb + p * hw + h))

    in_specs = [zspec(0), zspec(1), zspec(2), zspec(3), zspec(4),
                pl.BlockSpec((1, HEAD), lambda b, h: (0, 0))]
    args = [z, z, z, z, z, onorm_w]
    if s0 is not None:
        in_specs.append(pl.BlockSpec((None, None, 2, HG_HB, HEAD, HEAD), lambda b, h: (b, layer, 0, h, 0, 0)))
        args.append(s0)
    out_specs = [pl.BlockSpec((seq, bw), lambda b, h: (b, h))]
    out_shape = [jax.ShapeDtypeStruct((n_tok, HG_WIDTH), BF16)]
    aliases = {}
    if write_state:
        out_specs.append(pl.BlockSpec((None, None, 2, HG_HB, HEAD, HEAD), lambda b, h: (b, layer, 0, h, 0, 0)))
        out_shape.append(jax.ShapeDtypeStruct((nseq, depth, 2, HG_HEADS, HEAD, HEAD), F32))
        if states is not None:
            aliases = {len(args): 1}
            in_specs.append(pl.BlockSpec(memory_space=pl.ANY))
            args.append(states)
    res = pl.pallas_call(
        functools.partial(_hgrn_kernel, seq=seq, has_s0=s0 is not None, write_state=write_state),
        grid=(nseq, HG_HEADS // HG_HB),
        in_specs=in_specs, out_specs=out_specs, out_shape=out_shape,
        input_output_aliases=aliases,
        scratch_shapes=[pltpu.VMEM((2, seq, bw), F32), pltpu.VMEM((2, seq, bw), F32),
                        pltpu.VMEM((2, seq, bw), F32), pltpu.VMEM((2, HG_HB, HEAD, HEAD), F32),
                        pltpu.SMEM((2, seq // HG_CHUNK), F32)],
        compiler_params=pltpu.CompilerParams(
            dimension_semantics=("arbitrary", "arbitrary"), vmem_limit_bytes=VMEM_LIMIT),
        name="hgrn",
    )(*args)
    return (res[0], res[1]) if write_state else (res[0], None)


def _rope(x, cos, sin_signed, lane):
    swapped = jnp.where(lane % 64 < 32, pltpu.roll(x, 96, 1), pltpu.roll(x, 32, 1))
    return x * cos + swapped * sin_signed


def _attn_kernel(*refs, seq, past, ctx, qblk):
    q_ref, k_ref, v_ref, g_ref, qw_ref, kw_ref = refs[:6]
    pos = 6
    if ctx:
        ck_ref, cv_ref, cos_ref, sin_ref = refs[pos:pos + 4]
        o_ref = refs[-3]
    else:
        o_ref, nk_ref, nv_ref = refs[-5:-2]
    ks_ref, vs_ref = refs[-2:]

    k = _rms(k_ref[...], kw_ref[...])
    v = v_ref[...]
    vs_ref[:, HEAD:2 * HEAD] = jnp.ones((past + seq, HEAD), BF16)
    if ctx:
        lane = lax.broadcasted_iota(jnp.int32, (seq, HEAD), 1)
        ks_ref[0:past, :] = ck_ref[...].astype(BF16)
        vs_ref[0:past, 0:HEAD] = cv_ref[...].astype(BF16)
        k = _rope(k, cos_ref[...], sin_ref[...], lane)
    else:
        nk_ref[...] = k
        nv_ref[...] = v
    ks_ref[past:past + seq, :] = k.astype(BF16)
    vs_ref[past:past + seq, 0:HEAD] = v.astype(BF16)

    qscale = HEAD ** -0.5 * math.log2(math.e)
    qw = qw_ref[...]
    for hq in range(AT_GROUP):
        cols = slice(hq * HEAD, (hq + 1) * HEAD)
        for qb in range(seq // qblk):
            rows = slice(qb * qblk, (qb + 1) * qblk)
            q = _rms(q_ref[rows, cols], qw)
            if ctx:
                lane_q = lax.broadcasted_iota(jnp.int32, (qblk, HEAD), 1)
                q = _rope(q, cos_ref[rows, :], sin_ref[rows, :], lane_q)
            s = _dot_nt((q * qscale).astype(BF16), ks_ref[...])
            p = jnp.exp2(s - jnp.max(s, axis=-1, keepdims=True))
            res = _dot(p.astype(BF16), vs_ref[...])
            o = res[:, 0:HEAD] / res[:, HEAD:2 * HEAD]
            gt = g_ref[rows, cols]
            o_ref[rows, cols] = (o * (gt * _sigmoid(gt))).astype(BF16)


def _attention(z, q_norm, k_norm, ctx_k, ctx_v, cos, sin_signed, caches, layer, depth, *, seq):
    n_tok = z.shape[0]
    nseq = n_tok // seq
    ctx = ctx_k is not None
    past = ctx_k.shape[2] if ctx else 0
    gw = AT_GROUP * HEAD
    in_specs = [pl.BlockSpec((seq, gw), lambda b, h: (b, COL_ATQ // gw + h)),
                pl.BlockSpec((seq, HEAD), lambda b, h: (b, COL_ATK // HEAD + h)),
                pl.BlockSpec((seq, HEAD), lambda b, h: (b, COL_ATV // HEAD + h)),
                pl.BlockSpec((seq, gw), lambda b, h: (b, COL_ATG // gw + h)),
                pl.BlockSpec((1, HEAD), lambda b, h: (0, 0)),
                pl.BlockSpec((1, HEAD), lambda b, h: (0, 0))]
    args = [z, z, z, z, q_norm, k_norm]
    aliases = {}
    if ctx:
        kv_in = pl.BlockSpec((None, None, past, HEAD), lambda b, h: (b, layer, 0, h))
        in_specs += [kv_in, kv_in,
                     pl.BlockSpec((seq, HEAD), lambda b, h: (0, 0)),
                     pl.BlockSpec((seq, HEAD), lambda b, h: (0, 0))]
        args += [ctx_k, ctx_v, cos, sin_signed]
    out_specs = [pl.BlockSpec((seq, gw), lambda b, h: (b, h))]
    out_shape = [jax.ShapeDtypeStruct((n_tok, AT_KV_HEADS * gw), BF16)]
    if not ctx:
        kv_spec = pl.BlockSpec((None, None, seq, HEAD), lambda b, h: (b, layer, 0, h))
        out_specs += [kv_spec, kv_spec]
        out_shape += [jax.ShapeDtypeStruct((nseq, depth, seq, AT_KV_HEADS * HEAD), F32)] * 2
        if caches is not None:
            aliases = {len(args): 1, len(args) + 1: 2}
            in_specs += [pl.BlockSpec(memory_space=pl.ANY)] * 2
            args += list(caches)
    res = pl.pallas_call(
        functools.partial(_attn_kernel, seq=seq, past=past, ctx=ctx, qblk=min(seq, 256)),
        grid=(nseq, AT_KV_HEADS),
        in_specs=in_specs, out_specs=out_specs, out_shape=out_shape,
        input_output_aliases=aliases,
        scratch_shapes=[pltpu.VMEM((past + seq, HEAD), BF16), pltpu.VMEM((past + seq, 2 * HEAD), BF16)],
        compiler_params=pltpu.CompilerParams(
            dimension_semantics=("arbitrary", "arbitrary"), vmem_limit_bytes=VMEM_LIMIT),
        name="attention",
    )(*args)
    return (res[0], (res[1], res[2])) if not ctx else (res[0], None)


def _s5_scan_kernel(*refs, has_s0, write_state):
    u_ref, wb_ref, a_ref, wc_ref = refs[:4]
    pos = 4
    s0_ref = fin_ref = None
    if has_s0:
        s0_ref = refs[pos]
        pos += 1
    y_ref = refs[pos]
    pos += 1
    if write_state:
        fin_ref = refs[pos]
    carry_ref, s_ref = refs[-2:]

    d = pl.program_id(2)
    tc = pl.program_id(3)
    nt = pl.num_programs(3)
    half = S5_GB * S5_STATE
    uw = S5_GB * S5_GROUP_CH
    steps = S5_ROWS // S5_NB
    ps = steps // S5_PIECES
    pr = S5_ROWS // S5_PIECES

    @pl.when(tc == 0)
    def _():
        if has_s0:
            carry_ref[...] = s0_ref[...]
        else:
            carry_ref[...] = jnp.zeros_like(carry_ref)

    def run(rev):
        def fn():
            u = pltpu.einshape("btc->tbc", u_ref[...]).reshape(S5_ROWS, uw).astype(BF16)
            a_re = jnp.broadcast_to(a_ref[:, 0:half], (S5_NB, half))
            a_im = jnp.broadcast_to(a_ref[:, half:2 * half], (S5_NB, half))
            t0 = ((nt - 1 - tc) if rev else tc) * steps
            order = list(range(S5_PIECES))[::-1] if rev else list(range(S5_PIECES))

            def project_in(p):
                s_ref[p * pr:(p + 1) * pr, :] = _dot(u[p * pr:(p + 1) * pr, :], wb_ref[...])

            def recur(p, carry):
                sr, sm = carry
                ts = range(p * ps, (p + 1) * ps)
                for t in (reversed(ts) if rev else ts):
                    rows = slice(t * S5_NB, (t + 1) * S5_NB)
                    sr, sm = (a_re * sr - a_im * sm + s_ref[rows, 0:half],
                              a_re * sm + a_im * sr + s_ref[rows, half:2 * half])
                    s_ref[rows, 0:half] = sr
                    s_ref[rows, half:2 * half] = sm
                return sr, sm

            def project_out(p):
                y = _dot(s_ref[p * pr:(p + 1) * pr, :].astype(BF16), wc_ref[...])
                y = pltpu.einshape("tbc->btc", y.reshape(ps, S5_NB, uw))
                ts = pl.ds(pl.multiple_of(t0 + p * ps, ps), ps)
                if rev:
                    y_ref[:, ts, :] = y_ref[:, ts, :] + y
                else:
                    y_ref[:, ts, :] = y

            carry = (carry_ref[:, 0:half], carry_ref[:, half:2 * half])
            project_in(order[0])
            for i, p in enumerate(order):
                if i + 1 < S5_PIECES:
                    project_in(order[i + 1])
                carry = recur(p, carry)
                if i > 0:
                    project_out(order[i - 1])
            project_out(order[-1])
            carry_ref[:, 0:half] = carry[0]
            carry_ref[:, half:2 * half] = carry[1]
        return fn

    lax.cond(d == 0, run(False), run(True))

    if write_state:
        @pl.when(tc == nt - 1)
        def _():
            fin_ref[...] = carry_ref[...]


def _s5_scan(z, wb, a_bar, wc, s0, *, seq, write_state):
    n_tok = z.shape[0]
    nseq = n_tok // seq
    steps = S5_ROWS // S5_NB
    nt = seq // steps
    uw = S5_GB * S5_GROUP_CH
    sw = 2 * S5_GB * S5_STATE

    def tblk(d, t):
        return jnp.where(d == 0, t, nt - 1 - t)

    st_spec = pl.BlockSpec((None, None, S5_NB, sw), lambda g, b, d, t: (d, g, b, 0))
    in_specs = [pl.BlockSpec((S5_NB, steps, uw), lambda g, b, d, t: (b, tblk(d, t), COL_S5U // uw + g)),
                pl.BlockSpec((None, None, uw, sw), lambda g, b, d, t: (d, g, 0, 0)),
                pl.BlockSpec((None, None, 1, sw), lambda g, b, d, t: (d, g, 0, 0)),
                pl.BlockSpec((None, sw, uw), lambda g, b, d, t: (g, 0, 0))]
    args = [z.reshape(nseq, seq, IN_COLS), wb, a_bar, wc]
    if s0 is not None:
        in_specs.append(st_spec)
        args.append(s0)
    out_specs = [pl.BlockSpec((S5_NB, seq, uw), lambda g, b, d, t: (b, 0, g))]
    out_shape = [jax.ShapeDtypeStruct((nseq, seq, HG_WIDTH), F32)]
    if write_state:
        out_specs.append(st_spec)
        out_shape.append(jax.ShapeDtypeStruct((2, S5_NBLK, nseq, sw), F32))
    res = pl.pallas_call(
        functools.partial(_s5_scan_kernel, has_s0=s0 is not None, write_state=write_state),
        grid=(S5_NBLK, nseq // S5_NB, 2, nt),
        in_specs=in_specs, out_specs=out_specs, out_shape=out_shape,
        scratch_shapes=[pltpu.VMEM((S5_NB, sw), F32), pltpu.VMEM((S5_ROWS, sw), F32)],
        compiler_params=pltpu.CompilerParams(
            dimension_semantics=("arbitrary", "arbitrary", "arbitrary", "arbitrary"),
            vmem_limit_bytes=VMEM_LIMIT),
        name="s5_scan",
    )(*args)
    y = res[0].reshape(n_tok, HG_WIDTH)
    return (y, res[1]) if write_state else (y, None)


def _s5_post_kernel(y_ref, u_ref, g_ref, d_ref, w_ref, o_ref):
    y = y_ref[...] + d_ref[...] * u_ref[...]
    y = 0.5 * y * (1.0 + jnp.tanh(math.sqrt(2.0 / math.pi) * (y + 0.044715 * (y * y * y))))
    y = y * _sigmoid(_dot(y.astype(BF16), w_ref[...]))
    o_ref[...] = (y * g_ref[...]).astype(BF16)


def _s5_post(y, z, d_skip, w_glu, layer):
    n_tok = z.shape[0]
    w = HG_WIDTH
    tm = 1024
    return pl.pallas_call(
        _s5_post_kernel,
        grid=(n_tok // tm,),
        in_specs=[pl.BlockSpec((tm, w), lambda i: (i, 0)),
                  pl.BlockSpec((tm, w), lambda i: (i, COL_S5U // w)),
                  pl.BlockSpec((tm, w), lambda i: (i, COL_S5G // w)),
                  pl.BlockSpec((1, w), lambda i: (0, 0)),
                  pl.BlockSpec((None, w, w), lambda i: (layer, 0, 0))],
        out_specs=pl.BlockSpec((tm, w), lambda i: (i, 0)),
        out_shape=jax.ShapeDtypeStruct((n_tok, w), BF16),
        compiler_params=pltpu.CompilerParams(
            dimension_semantics=("arbitrary",), vmem_limit_bytes=VMEM_LIMIT),
        name="s5_post",
    )(y, z, z, d_skip, w_glu)


def _merge_kernel(ohg_ref, oat_ref, os5_ref, m0_ref, m1_ref, m2_ref, x_ref, mod_ref,
                  wh_ref, wa_ref, ws_ref, wo_ref, nw_ref, *rest, last):
    merged = _sigmoid(m0_ref[...]) * _dot(ohg_ref[...], wh_ref[...])
    merged = merged + _sigmoid(m1_ref[...]) * _dot(oat_ref[...], wa_ref[...])
    merged = merged + _sigmoid(m2_ref[...]) * _dot(os5_ref[...], ws_ref[...])
    out = _dot(merged.astype(BF16), wo_ref[...])
    x_new = x_ref[...] + mod_ref[2:3, :] * out
    if last:
        (y_ref,) = rest
        y_ref[...] = _rms(x_new, nw_ref[...])
    else:
        modn_ref, xo_ref, hn_ref = rest
        xo_ref[...] = x_new
        hn_ref[...] = _ada_norm(x_new, nw_ref[...], modn_ref).astype(BF16)


def _merge_out(o_hg, o_at, o_s5, z, x2, mod3, w_hg, w_at, w_s5, w_out, next_norm_w, mod3_next,
               layer, *, mod_row):
    n_tok, d = x2.shape
    w = HG_WIDTH
    tm = TM_OUT
    last = mod3_next is None
    act = pl.BlockSpec((tm, w), lambda i: (i, 0))
    row = pl.BlockSpec((tm, d), lambda i: (i, 0))
    mod_spec = pl.BlockSpec((None, 3, d), lambda i: (mod_row(i), 0, 0))

    def wspec(rows):
        return pl.BlockSpec((None, rows, d), lambda i: (layer, 0, 0), pipeline_mode=pl.Buffered(1))

    in_specs = [act, act, act,
                pl.BlockSpec((tm, d), lambda i: (i, 0)),
                pl.BlockSpec((tm, d), lambda i: (i, 1)),
                pl.BlockSpec((tm, d), lambda i: (i, 2)),
                row, mod_spec, wspec(w), wspec(w), wspec(w), wspec(d),
                pl.BlockSpec((1, d), lambda i: (0, 0))]
    args = [o_hg, o_at, o_s5, z, z, z, x2, mod3, w_hg, w_at, w_s5, w_out, next_norm_w]
    if last:
        out_specs = row
        out_shape = jax.ShapeDtypeStruct((n_tok, d), F32)
    else:
        in_specs.append(mod_spec)
        args.append(mod3_next)
        out_specs = [row, row]
        out_shape = [jax.ShapeDtypeStruct((n_tok, d), F32), jax.ShapeDtypeStruct((n_tok, d), BF16)]
    return pl.pallas_call(
        functools.partial(_merge_kernel, last=last),
        grid=(n_tok // tm,),
        in_specs=in_specs, out_specs=out_specs, out_shape=out_shape,
        compiler_params=pltpu.CompilerParams(
            dimension_semantics=("arbitrary",), vmem_limit_bytes=VMEM_LIMIT),
        name="merge_out",
    )(*args)


def _hgrn_lower_bounds(logits):
    p = jax.nn.softmax(logits.astype(F32), axis=1)
    cs = jnp.cumsum(p, axis=1)
    return cs - cs[:, :1]


def _s5_params(a_re, a_im, log_dt, b_re, b_im, c_re, c_im):
    dt = jnp.exp(log_dt)[..., None]
    mag = jnp.exp(a_re * dt)
    abar_re = mag * jnp.cos(a_im * dt)
    abar_im = mag * jnp.sin(a_im * dt)
    den = a_re * a_re + a_im * a_im
    num_re = abar_re - 1.0
    coef_re = ((num_re * a_re + abar_im * a_im) / den)[..., None]
    coef_im = ((abar_im * a_re - num_re * a_im) / den)[..., None]
    bbar_re = coef_re * b_re - coef_im * b_im
    bbar_im = coef_re * b_im + coef_im * b_re
    half = S5_GB * S5_STATE
    eye = jnp.eye(S5_GB, dtype=F32)
    a_bar = jnp.concatenate([abar_re.reshape(2, S5_NBLK, 1, half),
                             abar_im.reshape(2, S5_NBLK, 1, half)], axis=-1)

    def pack_b(bb):
        bb = bb.reshape(2, S5_NBLK, S5_GB, S5_STATE, S5_GROUP_CH).transpose(0, 1, 2, 4, 3)
        return jnp.einsum('dbgcp,gh->dbgchp', bb, eye).reshape(
            2, S5_NBLK, S5_GB * S5_GROUP_CH, half)

    def pack_c(cc):
        cc = cc.reshape(S5_NBLK, S5_GB, S5_GROUP_CH, S5_STATE).transpose(0, 1, 3, 2)
        return jnp.einsum('bgpc,gh->bgphc', cc, eye).reshape(S5_NBLK, half, S5_GB * S5_GROUP_CH)

    wb = jnp.concatenate([pack_b(bbar_re), pack_b(bbar_im)], axis=-1).astype(BF16)
    wc = jnp.concatenate([pack_c(c_re), -pack_c(c_im)], axis=1).astype(BF16)
    return a_bar, wb, wc


def _rope_tables(seq):
    rows = seq // GRID_W
    row = jnp.repeat(jnp.arange(rows, dtype=F32), GRID_W)
    col = jnp.tile(jnp.arange(GRID_W, dtype=F32), rows)
    quarter = HEAD // 4
    inv = ROPE_THETA ** (-jnp.arange(quarter, dtype=F32) / quarter)
    ang_r = row[:, None] * inv
    ang_c = col[:, None] * inv
    cos = jnp.concatenate([jnp.cos(ang_r)] * 2 + [jnp.cos(ang_c)] * 2, axis=-1)
    sin = jnp.concatenate([-jnp.sin(ang_r), jnp.sin(ang_r), -jnp.sin(ang_c), jnp.sin(ang_c)], axis=-1)
    return cos, sin


def _s5_state_in(s_re, s_im):
    def pack(s):
        b = s.shape[0]
        return s.reshape(b, 2, S5_NBLK, S5_GB * S5_STATE).transpose(1, 2, 0, 3)
    return jnp.concatenate([pack(s_re), pack(s_im)], axis=-1)


def _s5_state_out(fin):
    half = S5_GB * S5_STATE
    b = fin.shape[2]

    def unpack(s):
        return s.transpose(2, 0, 1, 3).reshape(b, 2, S5_GROUPS, S5_STATE)
    return unpack(fin[..., :half]), unpack(fin[..., half:])


def _trunk(x2, mod, mod_row, p, *, seq, depth, hg_s0, s5_s0, ctx_k, ctx_v, rope, want_state):
    hn = _first_norm(x2, mod[0], p['norm_w'][0], mod_row=mod_row)
    cos, sin = rope if rope is not None else (None, None)
    caches = hg_states = None
    if want_state:
        nseq = x2.shape[0] // seq
        hg_states = jnp.zeros((nseq, depth, 2, HG_HEADS, HEAD, HEAD), F32)
        caches = tuple(jnp.zeros((nseq, depth, seq, AT_KV_HEADS * HEAD), F32) for _ in range(2))
    s5_fins = []
    for l in range(depth):
        z = _in_proj(hn, p['w_in'], p['lower'], l)
        o_hg, hg_states = _hgrn(z, p['hg_onorm'][l], hg_s0, hg_states, l, depth,
                                seq=seq, write_state=want_state)
        o_at, caches = _attention(z, p['at_q_norm'][l], p['at_k_norm'][l], ctx_k, ctx_v, cos, sin, caches,
                                  l, depth, seq=seq)
        y, s5_fin = _s5_scan(z, p['s5'][l][1], p['s5'][l][0], p['s5'][l][2],
                             None if s5_s0 is None else s5_s0[l], seq=seq, write_state=want_state)
        s5_fins.append(s5_fin)
        o_s5 = _s5_post(y, z, p['s5_d'][l], p['s5_w_glu'], l)
        last = l == depth - 1
        res = _merge_out(o_hg, o_at, o_s5, z, x2, mod[l], p['w_br_hg'], p['w_br_at'], p['w_br_s5'],
                         p['w_out'], p['final_norm'] if last else p['norm_w'][l + 1],
                         None if last else mod[l + 1], l, mod_row=mod_row)
        if last:
            return res, caches, hg_states, s5_fins
        x2, hn = res


def kernel(x_prompt, x_sample, cache_k, cache_v, state_hgrn, state_s5_re, state_s5_im, c, c_ctx, norm_w, w_mod, b_mod, w_in, hg_lb_logits, hg_onorm, at_q_norm, at_k_norm, s5_a_re, s5_a_im, s5_log_dt, s5_b_re, s5_b_im, s5_c_re, s5_c_im, s5_d, s5_w_glu, w_br_hg, w_br_at, w_br_s5, w_out, final_norm):
    bsz_p, seq_p, d = x_prompt.shape
    bsz_s, seq_s, _ = x_sample.shape
    depth = w_in.shape[0]
    past = cache_k.shape[2]
    steps = S5_ROWS // S5_NB
    assert bsz_s == S5_NB and bsz_p % S5_NB == 0 and seq_p % steps == 0 and seq_s % steps == 0
    assert seq_p % HG_CHUNK == 0 and seq_s % HG_CHUNK == 0 and seq_p % TM_OUT == 0 and seq_s % TM_OUT == 0
    assert (bsz_p * seq_p) % TM_IN == 0 and (bsz_s * seq_s) % TM_IN == 0

    cvec = jnp.zeros((16, d), F32).at[:bsz_s].set(c).at[bsz_s].set(c_ctx)
    mod = _modulation(cvec, w_mod, b_mod).reshape(depth, 16, 3, d)

    lower = _hgrn_lower_bounds(hg_lb_logits)
    p = dict(
        norm_w=norm_w[:, None, :], final_norm=final_norm[None, :], w_in=w_in,
        lower=jnp.concatenate([lower[0], lower[1]], axis=-1)[:, None, :], hg_onorm=hg_onorm[:, None, :],
        at_q_norm=at_q_norm[:, None, :], at_k_norm=at_k_norm[:, None, :],
        s5=[_s5_params(s5_a_re[l], s5_a_im[l], s5_log_dt[l], s5_b_re[l], s5_b_im[l], s5_c_re[l], s5_c_im[l])
            for l in range(depth)],
        s5_d=s5_d[:, None, :], s5_w_glu=s5_w_glu.astype(BF16),
        w_br_hg=w_br_hg.astype(BF16), w_br_at=w_br_at.astype(BF16), w_br_s5=w_br_s5.astype(BF16),
        w_out=w_out.astype(BF16))

    yp, caches, hg_states, s5_fins = _trunk(
        x_prompt.reshape(bsz_p * seq_p, d), mod, lambda i: bsz_s, p, seq=seq_p, depth=depth,
        hg_s0=None, s5_s0=None, ctx_k=None, ctx_v=None, rope=None, want_state=True)
    y_prompt = yp.reshape(bsz_p, seq_p, d)
    new_cache_k = caches[0].reshape(bsz_p, depth, seq_p, AT_KV_HEADS, HEAD)
    new_cache_v = caches[1].reshape(bsz_p, depth, seq_p, AT_KV_HEADS, HEAD)
    s5s = [_s5_state_out(f) for f in s5_fins]
    new_state_s5_re = jnp.stack([s[0] for s in s5s], axis=1)
    new_state_s5_im = jnp.stack([s[1] for s in s5s], axis=1)

    tiles_per_seq = seq_s // TM_OUT
    ys, _, _, _ = _trunk(
        x_sample.reshape(bsz_s * seq_s, d), mod, lambda i: i // tiles_per_seq, p, seq=seq_s, depth=depth,
        hg_s0=state_hgrn,
        s5_s0=[_s5_state_in(state_s5_re[:, l], state_s5_im[:, l]) for l in range(depth)],
        ctx_k=cache_k.reshape(bsz_s, depth, past, AT_KV_HEADS * HEAD),
        ctx_v=cache_v.reshape(bsz_s, depth, past, AT_KV_HEADS * HEAD),
        rope=_rope_tables(seq_s), want_state=False)
    y_sample = ys.reshape(bsz_s, seq_s, d)

    return (y_prompt, y_sample, new_cache_k, new_cache_v, hg_states, new_state_s5_re, new_state_s5_im)
```

```python
import functools
import math

import jax
import jax.numpy as jnp
from jax import lax
from jax.experimental import pallas as pl
from jax.experimental.pallas import tpu as pltpu

F32 = jnp.float32
BF16 = jnp.bfloat16
EPS = 1e-6

D_MODEL = 2048
HG_WIDTH = 1024
HEAD = 128
HG_HEADS = 8
AT_KV_HEADS = 2
AT_GROUP = 4
S5_GROUPS = 64
S5_STATE = 64
S5_GROUP_CH = 16
GRID_W = 64
ROPE_THETA = 10000.0

COL_MERGE = 0
COL_HG = 6144
COL_S5U = 11264
COL_S5G = 12288
COL_ATQ = 13312
COL_ATG = 14336
COL_ATK = 15360
COL_ATV = 15616
IN_COLS = 15872

TM_IN = 2048
TN_IN = 512
TM_OUT = 256
HG_CHUNK = 64
HG_HB = 4
HG_FAST_MIN_LOG_DECAY = -60.0
S5_GB = 16
S5_NBLK = S5_GROUPS // S5_GB
S5_NB = 8
S5_ROWS = 2048
S5_PIECES = 8
VMEM_LIMIT = 56 * 1024 * 1024


def _sigmoid(x):
    return 1.0 / (1.0 + jnp.exp(-x))


def _dot(a, b):
    return jnp.dot(a, b, preferred_element_type=F32)


def _dot_nt(a, b):
    return lax.dot_general(a, b, (((1,), (1,)), ((), ())), preferred_element_type=F32)


def _dot_tn(a, b):
    return lax.dot_general(a, b, (((0,), (0,)), ((), ())), preferred_element_type=F32)


def _rms(x, w):
    ms = jnp.mean(x * x, axis=-1, keepdims=True)
    return x * lax.rsqrt(ms + EPS) * w


def _ada_norm(x, nw, mod_ref):
    return _rms(x, nw) * (1.0 + mod_ref[1:2, :]) + mod_ref[0:1, :]


def _mod_kernel(c_ref, w_ref, b_ref, o_ref):
    c = c_ref[...]
    s = (c * _sigmoid(c)).astype(BF16)
    o_ref[...] = _dot(s, w_ref[...].astype(BF16)) + b_ref[...]


def _modulation(cvec, w_mod, b_mod):
    depth, d, n = w_mod.shape
    tn = 1024
    return pl.pallas_call(
        _mod_kernel,
        grid=(depth, n // tn),
        in_specs=[pl.BlockSpec((16, d), lambda l, j: (0, 0)),
                  pl.BlockSpec((None, d, tn), lambda l, j: (l, 0, j)),
                  pl.BlockSpec((None, 1, tn), lambda l, j: (l, 0, j))],
        out_specs=pl.BlockSpec((None, 16, tn), lambda l, j: (l, 0, j)),
        out_shape=jax.ShapeDtypeStruct((depth, 16, n), F32),
        compiler_params=pltpu.CompilerParams(
            dimension_semantics=("arbitrary", "arbitrary"), vmem_limit_bytes=VMEM_LIMIT),
        name="modulation",
    )(cvec, w_mod, b_mod.reshape(depth, 1, n))


def _norm_kernel(x_ref, mod_ref, nw_ref, h_ref):
    h_ref[...] = _ada_norm(x_ref[...], nw_ref[...], mod_ref).astype(BF16)


def _first_norm(x2, mod3, norm_w, *, mod_row):
    n_tok, d = x2.shape
    tm = TM_OUT
    return pl.pallas_call(
        _norm_kernel,
        grid=(n_tok // tm,),
        in_specs=[pl.BlockSpec((tm, d), lambda i: (i, 0)),
                  pl.BlockSpec((None, 3, d), lambda i: (mod_row(i), 0, 0)),
                  pl.BlockSpec((1, d), lambda i: (0, 0))],
        out_specs=pl.BlockSpec((tm, d), lambda i: (i, 0)),
        out_shape=jax.ShapeDtypeStruct((n_tok, d), BF16),
        compiler_params=pltpu.CompilerParams(dimension_semantics=("arbitrary",)),
        name="first_norm",
    )(x2, mod3, norm_w)


def _w_in_tile(j):
    return jnp.where(j < 12, j + 19,
           jnp.where(j < 22, j - 12,
           jnp.where(j < 26, j - 7,
           jnp.where(j < 28, j - 16,
           jnp.where(j < 30, j - 15, 12)))))


TILE_FORGET0 = (COL_HG + 2 * HG_WIDTH) // TN_IN


def _in_tile_kind(j):
    def tiles(col):
        return col // TN_IN, (col + HG_WIDTH) // TN_IN

    def within(lo_hi):
        return (j >= lo_hi[0]) & (j < lo_hi[1])

    silu = within(tiles(COL_HG + 4 * HG_WIDTH)) | within(tiles(COL_S5G))
    forget = (j >= TILE_FORGET0) & (j < TILE_FORGET0 + 2 * HG_WIDTH // TN_IN)
    return jnp.where(forget, 1, jnp.where(silu, 2, 0))


def _in_proj_kernel(h_ref, w_ref, lb_ref, z_ref):
    def branch(kind):
        def fn():
            z = _dot(h_ref[...], w_ref[...].astype(BF16))
            if kind == 1:
                lb = lb_ref[...]
                z = lb + (1.0 - lb) * _sigmoid(z)
            elif kind == 2:
                z = z * _sigmoid(z)
            z_ref[...] = z
        return fn

    lax.switch(_in_tile_kind(pl.program_id(1)), [branch(kind) for kind in range(3)])


def _in_proj(hn, w_in, lower, layer):
    n_tok, d = hn.shape
    n_forget = 2 * HG_WIDTH // TN_IN
    return pl.pallas_call(
        _in_proj_kernel,
        grid=(n_tok // TM_IN, IN_COLS // TN_IN),
        in_specs=[pl.BlockSpec((TM_IN, d), lambda i, j: (i, 0)),
                  pl.BlockSpec((None, d, TN_IN), lambda i, j: (layer, 0, _w_in_tile(j))),
                  pl.BlockSpec((None, 1, TN_IN),
                               lambda i, j: (layer, 0, jnp.clip(j - TILE_FORGET0, 0, n_forget - 1)))],
        out_specs=pl.BlockSpec((TM_IN, TN_IN), lambda i, j: (i, j)),
        out_shape=jax.ShapeDtypeStruct((n_tok, IN_COLS), F32),
        compiler_params=pltpu.CompilerParams(
            dimension_semantics=("arbitrary", "arbitrary"), vmem_limit_bytes=VMEM_LIMIT),
        name="in_proj",
    )(hn, w_in, lower)


def _ref_rows(b, bs, ridx, sub8):
    c = b.shape[0]
    if bs >= 8:
        b3 = b.reshape(c // bs, bs, HEAD)
        return jnp.broadcast_to(b3[:, ridx:ridx + 1, :], (c // bs, bs, HEAD)).reshape(c, HEAD)
    b3 = b.reshape(c // 8, 8, HEAD)
    r = None
    for blk in range(8 // bs - 1, -1, -1):
        row = jnp.broadcast_to(b3[:, blk * bs + ridx:blk * bs + ridx + 1, :], (c // 8, 8, HEAD))
        r = row if r is None else jnp.where(sub8 < (blk + 1) * bs, row, r)
    return r.reshape(c, HEAD)


def _hgrn_scores_robust(q, k, f, b, rev, keep, xor, sub8):
    c = b.shape[0]
    scores = None
    for lvl in range(int(math.log2(c)), 1, -1):
        bs = 1 << lvl
        r = _ref_rows(b, bs, bs // 2 if rev else bs // 2 - 1, sub8)
        x = b - r
        e = jnp.exp(-jnp.abs(x))
        qe = jnp.where(x <= 0.0, e, 1.0)
        ke = jnp.where(x >= 0.0, e, 1.0)
        s_l = _dot_nt((q * qe).astype(BF16), (k * ke).astype(BF16))
        scores = s_l if scores is None else jnp.where(xor < bs, s_l, scores)
    s_l = _dot_nt((q * f).astype(BF16), k.astype(BF16))
    scores = jnp.where(xor < 2, s_l, scores)
    diag = jnp.sum(q * k, axis=1, keepdims=True)
    scores = jnp.where(xor == 0, diag, scores)
    return jnp.where(keep, scores, 0.0)


def _cumsum3(tri, g):
    w = g.shape[1]
    hi = g.astype(BF16)
    r1 = g - hi.astype(F32)
    mid = r1.astype(BF16)
    lo = (r1 - mid.astype(F32)).astype(BF16)
    r = _dot(tri, jnp.concatenate([hi, mid, lo], axis=1))
    return r[:, 0:w] + r[:, w:2 * w] + r[:, 2 * w:3 * w]


def _cumprod_rows(f, rev, row):
    c = f.shape[0]
    p = f
    s = 1
    while s < c:
        shifted = pltpu.roll(p, (c - s) if rev else s, 0)
        valid = (row < c - s) if rev else (row >= s)
        p = p * jnp.where(valid, shifted, 1.0)
        s *= 2
    return p


def _hgrn_kernel(*refs, seq, has_s0, write_state):
    q_ref, v_ref, ff_ref, fb_ref, g_ref, onw_ref = refs[:6]
    pos = 6
    s0_ref = st_ref = None
    if has_s0:
        s0_ref = refs[pos]
        pos += 1
    acc_ref, st_scr, low_ref = refs[-3:]
    if write_state:
        o_ref, st_ref = refs[-5:-3]
    else:
        o_ref = refs[-4]
    f_refs = (ff_ref, fb_ref)

    c = HG_CHUNK
    n = seq // c
    bw = HG_HB * HEAD
    ti = lax.broadcasted_iota(jnp.int32, (c, c), 0)
    si = lax.broadcasted_iota(jnp.int32, (c, c), 1)
    xor = jnp.bitwise_xor(ti, si)
    sub8 = lax.broadcasted_iota(jnp.int32, (c // 8, 8, HEAD), 1)
    row = lax.broadcasted_iota(jnp.int32, (c, bw), 0)
    keeps = (si <= ti, si >= ti)
    tris = tuple(jnp.where(kp, 1.0, 0.0).astype(BF16) for kp in keeps)
    heads = [slice(h * HEAD, (h + 1) * HEAD) for h in range(HG_HB)]

    for d in range(2):
        for i in range(n):
            f3 = f_refs[d][i * c:(i + 1) * c, :].reshape(c // 8, 8, bw)
            prod = f3[0]
            for j in range(1, c // 8):
                prod = prod * f3[j]
            for s in (4, 2, 1):
                prod = prod * pltpu.roll(prod, s, 0)
            low_ref[d, i] = jnp.min(prod)
        for h in range(HG_HB):
            if has_s0:
                st_scr[d, h] = s0_ref[d, h].T
            else:
                st_scr[d, h] = jnp.zeros((HEAD, HEAD), F32)

    def body(ci, carry):
        rows = (pl.ds(pl.multiple_of(ci * c, c), c), pl.ds(pl.multiple_of((n - 1 - ci) * c, c), c))

        def run(fast):
            def fn():
                q = [q_ref[rows[d], :] for d in range(2)]
                f = [f_refs[d][rows[d], :] for d in range(2)]
                k = [1.0 - f[d] for d in range(2)]
                vb = [v_ref[rows[d], :].astype(BF16) for d in range(2)]
                scores = [[None] * HG_HB for _ in range(2)]
                if fast:
                    cum = [_cumprod_rows(f[d], bool(d), row) for d in range(2)]
                    dec = [cum[0][c - 1:c, :], cum[1][0:1, :]]
                    qd = [(q[d] * cum[d]).astype(BF16) for d in range(2)]
                    kinv = [k[d] / cum[d] for d in range(2)]
                    kb = [kinv[d].astype(BF16) for d in range(2)]
                    kd = [(kinv[d] * dec[d]).astype(BF16) for d in range(2)]
                    for d in range(2):
                        for h, hs in enumerate(heads):
                            scores[d][h] = jnp.where(keeps[d], _dot_nt(qd[d][:, hs], kb[d][:, hs]), 0.0)
                else:
                    b = [_cumsum3(tris[d], jnp.log(f[d])) for d in range(2)]
                    edge = [b[0][c - 1:c, :], b[1][0:1, :]]
                    dec = [jnp.exp(edge[d]) for d in range(2)]
                    qd = [(q[d] * jnp.exp(b[d])).astype(BF16) for d in range(2)]
                    kd = [(k[d] * jnp.exp(edge[d] - b[d])).astype(BF16) for d in range(2)]
                    for d in range(2):
                        for h, hs in enumerate(heads):
                            scores[d][h] = _hgrn_scores_robust(
                                q[d][:, hs], k[d][:, hs], f[d][:, hs], b[d][:, hs], bool(d),
                                keeps[d], xor, sub8)
                for d in range(2):
                    for h, hs in enumerate(heads):
                        st = st_scr[d, h]
                        o = _dot(scores[d][h].astype(BF16), vb[d][:, hs]) + _dot_nt(
                            qd[d][:, hs], st.astype(BF16))
                        st_new = st * dec[d][:, hs] + _dot_tn(vb[d][:, hs], kd[d][:, hs])
                        st_scr[d, h] = st_new
                        acc_ref[d, rows[d], hs] = o
            return fn

        low = jnp.minimum(low_ref[0, ci], low_ref[1, n - 1 - ci])
        lax.cond(low >= math.exp(HG_FAST_MIN_LOG_DECAY), run(True), run(False))
        return carry

    lax.fori_loop(0, n, body, 0)
    if write_state:
        for d in range(2):
            for h in range(HG_HB):
                st_ref[d, h] = st_scr[d, h].T
    onw = onw_ref[...]
    for hs in heads:
        y = _rms(acc_ref[0, :, hs] + acc_ref[1, :, hs], onw)
        o_ref[:, hs] = (y * g_ref[:, hs]).astype(BF16)


def _hgrn(z, onorm_w, s0, states, layer, depth, *, seq, write_state):
    n_tok = z.shape[0]
    nseq = n_tok // seq
    bw = HG_HB * HEAD
    cb = COL_HG // bw
    hw = HG_WIDTH // bw

    def zspec(piece):
        return pl.BlockSpec((seq, bw), lambda b, h, p=piece: (b, cb + p * hw + h))

    in_specs = [zspec(0), zspec(1), zspec(2), zspec(3), zspec(4),
                pl.BlockSpec((1, HEAD), lambda b, h: (0, 0))]
    args = [z, z, z, z, z, onorm_w]
    if s0 is not None:
        in_specs.append(pl.BlockSpec((None, None, 2, HG_HB, HEAD, HEAD), lambda b, h: (b, layer, 0, h, 0, 0)))
        args.append(s0)
    out_specs = [pl.BlockSpec((seq, bw), lambda b, h: (b, h))]
    out_shape = [jax.ShapeDtypeStruct((n_tok, HG_WIDTH), BF16)]
    aliases = {}
    if write_state:
        out_specs.append(pl.BlockSpec((None, None, 2, HG_HB, HEAD, HEAD), lambda b, h: (b, layer, 0, h, 0, 0)))
        out_shape.append(jax.ShapeDtypeStruct((nseq, depth, 2, HG_HEADS, HEAD, HEAD), F32))
        if states is not None:
            aliases = {len(args): 1}
            in_specs.append(pl.BlockSpec(memory_space=pl.ANY))
            args.append(states)
    res = pl.pallas_call(
        functools.partial(_hgrn_kernel, seq=seq, has_s0=s0 is not None, write_state=write_state),
        grid=(nseq, HG_HEADS // HG_HB),
        in_specs=in_specs, out_specs=out_specs, out_shape=out_shape,
        input_output_aliases=aliases,
        scratch_shapes=[pltpu.VMEM((2, seq, bw), F32), pltpu.VMEM((2, HG_HB, HEAD, HEAD), F32),
                        pltpu.SMEM((2, seq // HG_CHUNK), F32)],
        compiler_params=pltpu.CompilerParams(
            dimension_semantics=("arbitrary", "arbitrary"), vmem_limit_bytes=VMEM_LIMIT),
        name="hgrn",
    )(*args)
    return (res[0], res[1]) if write_state else (res[0], None)


def _rope(x, cos, sin_signed, lane):
    swapped = jnp.where(lane % 64 < 32, pltpu.roll(x, 96, 1), pltpu.roll(x, 32, 1))
    return x * cos + swapped * sin_signed


def _attn_kernel(*refs, seq, past, ctx, qblk):
    q_ref, k_ref, v_ref, g_ref, qw_ref, kw_ref = refs[:6]
    pos = 6
    if ctx:
        ck_ref, cv_ref, cos_ref, sin_ref = refs[pos:pos + 4]
        o_ref = refs[-3]
    else:
        o_ref, nk_ref, nv_ref = refs[-5:-2]
    ks_ref, vs_ref = refs[-2:]

    k = _rms(k_ref[...], kw_ref[...])
    v = v_ref[...]
    vs_ref[:, HEAD:2 * HEAD] = jnp.ones((past + seq, HEAD), BF16)
    if ctx:
        lane = lax.broadcasted_iota(jnp.int32, (seq, HEAD), 1)
        ks_ref[0:past, :] = ck_ref[...].astype(BF16)
        vs_ref[0:past, 0:HEAD] = cv_ref[...].astype(BF16)
        k = _rope(k, cos_ref[...], sin_ref[...], lane)
    else:
        nk_ref[...] = k
        nv_ref[...] = v
    ks_ref[past:past + seq, :] = k.astype(BF16)
    vs_ref[past:past + seq, 0:HEAD] = v.astype(BF16)

    qscale = HEAD ** -0.5 * math.log2(math.e)
    qw = qw_ref[...]
    for hq in range(AT_GROUP):
        cols = slice(hq * HEAD, (hq + 1) * HEAD)
        for qb in range(seq // qblk):
            rows = slice(qb * qblk, (qb + 1) * qblk)
            q = _rms(q_ref[rows, cols], qw)
            if ctx:
                lane_q = lax.broadcasted_iota(jnp.int32, (qblk, HEAD), 1)
                q = _rope(q, cos_ref[rows, :], sin_ref[rows, :], lane_q)
            s = _dot_nt((q * qscale).astype(BF16), ks_ref[...])
            p = jnp.exp2(s - jnp.max(s, axis=-1, keepdims=True))
            res = _dot(p.astype(BF16), vs_ref[...])
            o = res[:, 0:HEAD] / res[:, HEAD:2 * HEAD]
            gt = g_ref[rows, cols]
            o_ref[rows, cols] = (o * (gt * _sigmoid(gt))).astype(BF16)


def _attention(z, q_norm, k_norm, ctx_k, ctx_v, cos, sin_signed, caches, layer, depth, *, seq):
    n_tok = z.shape[0]
    nseq = n_tok // seq
    ctx = ctx_k is not None
    past = ctx_k.shape[2] if ctx else 0
    gw = AT_GROUP * HEAD
    in_specs = [pl.BlockSpec((seq, gw), lambda b, h: (b, COL_ATQ // gw + h)),
                pl.BlockSpec((seq, HEAD), lambda b, h: (b, COL_ATK // HEAD + h)),
                pl.BlockSpec((seq, HEAD), lambda b, h: (b, COL_ATV // HEAD + h)),
                pl.BlockSpec((seq, gw), lambda b, h: (b, COL_ATG // gw + h)),
                pl.BlockSpec((1, HEAD), lambda b, h: (0, 0)),
                pl.BlockSpec((1, HEAD), lambda b, h: (0, 0))]
    args = [z, z, z, z, q_norm, k_norm]
    aliases = {}
    if ctx:
        kv_in = pl.BlockSpec((None, None, past, HEAD), lambda b, h: (b, layer, 0, h))
        in_specs += [kv_in, kv_in,
                     pl.BlockSpec((seq, HEAD), lambda b, h: (0, 0)),
                     pl.BlockSpec((seq, HEAD), lambda b, h: (0, 0))]
        args += [ctx_k, ctx_v, cos, sin_signed]
    out_specs = [pl.BlockSpec((seq, gw), lambda b, h: (b, h))]
    out_shape = [jax.ShapeDtypeStruct((n_tok, AT_KV_HEADS * gw), BF16)]
    if not ctx:
        kv_spec = pl.BlockSpec((None, None, seq, HEAD), lambda b, h: (b, layer, 0, h))
        out_specs += [kv_spec, kv_spec]
        out_shape += [jax.ShapeDtypeStruct((nseq, depth, seq, AT_KV_HEADS * HEAD), F32)] * 2
        if caches is not None:
            aliases = {len(args): 1, len(args) + 1: 2}
            in_specs += [pl.BlockSpec(memory_space=pl.ANY)] * 2
            args += list(caches)
    res = pl.pallas_call(
        functools.partial(_attn_kernel, seq=seq, past=past, ctx=ctx, qblk=min(seq, 256)),
        grid=(nseq, AT_KV_HEADS),
        in_specs=in_specs, out_specs=out_specs, out_shape=out_shape,
        input_output_aliases=aliases,
        scratch_shapes=[pltpu.VMEM((past + seq, HEAD), BF16), pltpu.VMEM((past + seq, 2 * HEAD), BF16)],
        compiler_params=pltpu.CompilerParams(
            dimension_semantics=("arbitrary", "arbitrary"), vmem_limit_bytes=VMEM_LIMIT),
        name="attention",
    )(*args)
    return (res[0], (res[1], res[2])) if not ctx else (res[0], None)


def _s5_scan_kernel(*refs, has_s0, write_state):
    u_ref, wb_ref, a_ref, wc_ref = refs[:4]
    pos = 4
    s0_ref = fin_ref = None
    if has_s0:
        s0_ref = refs[pos]
        pos += 1
    y_ref = refs[pos]
    pos += 1
    if write_state:
        fin_ref = refs[pos]
    carry_ref, s_ref = refs[-2:]

    d = pl.program_id(2)
    tc = pl.program_id(3)
    nt = pl.num_programs(3)
    half = S5_GB * S5_STATE
    uw = S5_GB * S5_GROUP_CH
    steps = S5_ROWS // S5_NB
    ps = steps // S5_PIECES
    pr = S5_ROWS // S5_PIECES

    @pl.when(tc == 0)
    def _():
        if has_s0:
            carry_ref[...] = s0_ref[...]
        else:
            carry_ref[...] = jnp.zeros_like(carry_ref)

    def run(rev):
        def fn():
            u = pltpu.einshape("btc->tbc", u_ref[...]).reshape(S5_ROWS, uw).astype(BF16)
            a_re = jnp.broadcast_to(a_ref[:, 0:half], (S5_NB, half))
            a_im = jnp.broadcast_to(a_ref[:, half:2 * half], (S5_NB, half))
            t0 = ((nt - 1 - tc) if rev else tc) * steps
            order = list(range(S5_PIECES))[::-1] if rev else list(range(S5_PIECES))

            def project_in(p):
                s_ref[p * pr:(p + 1) * pr, :] = _dot(u[p * pr:(p + 1) * pr, :], wb_ref[...])

            def recur(p, carry):
                sr, sm = carry
                ts = range(p * ps, (p + 1) * ps)
                for t in (reversed(ts) if rev else ts):
                    rows = slice(t * S5_NB, (t + 1) * S5_NB)
                    sr, sm = (a_re * sr - a_im * sm + s_ref[rows, 0:half],
                              a_re * sm + a_im * sr + s_ref[rows, half:2 * half])
                    s_ref[rows, 0:half] = sr
                    s_ref[rows, half:2 * half] = sm
                return sr, sm

            def project_out(p):
                y = _dot(s_ref[p * pr:(p + 1) * pr, :].astype(BF16), wc_ref[...])
                y = pltpu.einshape("tbc->btc", y.reshape(ps, S5_NB, uw))
                ts = pl.ds(pl.multiple_of(t0 + p * ps, ps), ps)
                if rev:
                    y_ref[:, ts, :] = y_ref[:, ts, :] + y
                else:
                    y_ref[:, ts, :] = y

            carry = (carry_ref[:, 0:half], carry_ref[:, half:2 * half])
            project_in(order[0])
            for i, p in enumerate(order):
                if i + 1 < S5_PIECES:
                    project_in(order[i + 1])
                carry = recur(p, carry)
                if i > 0:
                    project_out(order[i - 1])
            project_out(order[-1])
            carry_ref[:, 0:half] = carry[0]
            carry_ref[:, half:2 * half] = carry[1]
        return fn

    lax.cond(d == 0, run(False), run(True))

    if write_state:
        @pl.when(tc == nt - 1)
        def _():
            fin_ref[...] = carry_ref[...]


def _s5_scan(z, wb, a_bar, wc, s0, *, seq, write_state):
    n_tok = z.shape[0]
    nseq = n_tok // seq
    steps = S5_ROWS // S5_NB
    nt = seq // steps
    uw = S5_GB * S5_GROUP_CH
    sw = 2 * S5_GB * S5_STATE

    def tblk(d, t):
        return jnp.where(d == 0, t, nt - 1 - t)

    st_spec = pl.BlockSpec((None, None, S5_NB, sw), lambda g, b, d, t: (d, g, b, 0))
    in_specs = [pl.BlockSpec((S5_NB, steps, uw), lambda g, b, d, t: (b, tblk(d, t), COL_S5U // uw + g)),
                pl.BlockSpec((None, None, uw, sw), lambda g, b, d, t: (d, g, 0, 0)),
                pl.BlockSpec((None, None, 1, sw), lambda g, b, d, t: (d, g, 0, 0)),
                pl.BlockSpec((None, sw, uw), lambda g, b, d, t: (g, 0, 0))]
    args = [z.reshape(nseq, seq, IN_COLS), wb, a_bar, wc]
    if s0 is not None:
        in_specs.append(st_spec)
        args.append(s0)
    out_specs = [pl.BlockSpec((S5_NB, seq, uw), lambda g, b, d, t: (b, 0, g))]
    out_shape = [jax.ShapeDtypeStruct((nseq, seq, HG_WIDTH), F32)]
    if write_state:
        out_specs.append(st_spec)
        out_shape.append(jax.ShapeDtypeStruct((2, S5_NBLK, nseq, sw), F32))
    res = pl.pallas_call(
        functools.partial(_s5_scan_kernel, has_s0=s0 is not None, write_state=write_state),
        grid=(S5_NBLK, nseq // S5_NB, 2, nt),
        in_specs=in_specs, out_specs=out_specs, out_shape=out_shape,
        scratch_shapes=[pltpu.VMEM((S5_NB, sw), F32), pltpu.VMEM((S5_ROWS, sw), F32)],
        compiler_params=pltpu.CompilerParams(
            dimension_semantics=("arbitrary", "arbitrary", "arbitrary", "arbitrary"),
            vmem_limit_bytes=VMEM_LIMIT),
        name="s5_scan",
    )(*args)
    y = res[0].reshape(n_tok, HG_WIDTH)
    return (y, res[1]) if write_state else (y, None)


def _s5_post_kernel(y_ref, u_ref, g_ref, d_ref, w_ref, o_ref):
    y = y_ref[...] + d_ref[...] * u_ref[...]
    y = 0.5 * y * (1.0 + jnp.tanh(math.sqrt(2.0 / math.pi) * (y + 0.044715 * (y * y * y))))
    y = y * _sigmoid(_dot(y.astype(BF16), w_ref[...]))
    o_ref[...] = (y * g_ref[...]).astype(BF16)


def _s5_post(y, z, d_skip, w_glu, layer):
    n_tok = z.shape[0]
    w = HG_WIDTH
    tm = 1024
    return pl.pallas_call(
        _s5_post_kernel,
        grid=(n_tok // tm,),
        in_specs=[pl.BlockSpec((tm, w), lambda i: (i, 0)),
                  pl.BlockSpec((tm, w), lambda i: (i, COL_S5U // w)),
                  pl.BlockSpec((tm, w), lambda i: (i, COL_S5G // w)),
                  pl.BlockSpec((1, w), lambda i: (0, 0)),
                  pl.BlockSpec((None, w, w), lambda i: (layer, 0, 0))],
        out_specs=pl.BlockSpec((tm, w), lambda i: (i, 0)),
        out_shape=jax.ShapeDtypeStruct((n_tok, w), BF16),
        compiler_params=pltpu.CompilerParams(
            dimension_semantics=("arbitrary",), vmem_limit_bytes=VMEM_LIMIT),
        name="s5_post",
    )(y, z, z, d_skip, w_glu)


def _merge_kernel(ohg_ref, oat_ref, os5_ref, m0_ref, m1_ref, m2_ref, x_ref, mod_ref,
                  wh_ref, wa_ref, ws_ref, wo_ref, nw_ref, *rest, last):
    merged = _sigmoid(m0_ref[...]) * _dot(ohg_ref[...], wh_ref[...])
    merged = merged + _sigmoid(m1_ref[...]) * _dot(oat_ref[...], wa_ref[...])
    merged = merged + _sigmoid(m2_ref[...]) * _dot(os5_ref[...], ws_ref[...])
    out = _dot(merged.astype(BF16), wo_ref[...])
    x_new = x_ref[...] + mod_ref[2:3, :] * out
    if last:
        (y_ref,) = rest
        y_ref[...] = _rms(x_new, nw_ref[...])
    else:
        modn_ref, xo_ref, hn_ref = rest
        xo_ref[...] = x_new
        hn_ref[...] = _ada_norm(x_new, nw_ref[...], modn_ref).astype(BF16)


def _merge_out(o_hg, o_at, o_s5, z, x2, mod3, w_hg, w_at, w_s5, w_out, next_norm_w, mod3_next,
               layer, *, mod_row):
    n_tok, d = x2.shape
    w = HG_WIDTH
    tm = TM_OUT
    last = mod3_next is None
    act = pl.BlockSpec((tm, w), lambda i: (i, 0))
    row = pl.BlockSpec((tm, d), lambda i: (i, 0))
    mod_spec = pl.BlockSpec((None, 3, d), lambda i: (mod_row(i), 0, 0))

    def wspec(rows):
        return pl.BlockSpec((None, rows, d), lambda i: (layer, 0, 0), pipeline_mode=pl.Buffered(1))

    in_specs = [act, act, act,
                pl.BlockSpec((tm, d), lambda i: (i, 0)),
                pl.BlockSpec((tm, d), lambda i: (i, 1)),
                pl.BlockSpec((tm, d), lambda i: (i, 2)),
                row, mod_spec, wspec(w), wspec(w), wspec(w), wspec(d),
                pl.BlockSpec((1, d), lambda i: (0, 0))]
    args = [o_hg, o_at, o_s5, z, z, z, x2, mod3, w_hg, w_at, w_s5, w_out, next_norm_w]
    if last:
        out_specs = row
        out_shape = jax.ShapeDtypeStruct((n_tok, d), F32)
    else:
        in_specs.append(mod_spec)
        args.append(mod3_next)
        out_specs = [row, row]
        out_shape = [jax.ShapeDtypeStruct((n_tok, d), F32), jax.ShapeDtypeStruct((n_tok, d), BF16)]
    return pl.pallas_call(
        functools.partial(_merge_kernel, last=last),
        grid=(n_tok // tm,),
        in_specs=in_specs, out_specs=out_specs, out_shape=out_shape,
        compiler_params=pltpu.CompilerParams(
            dimension_semantics=("arbitrary",), vmem_limit_bytes=VMEM_LIMIT),
        name="merge_out",
    )(*args)


def _hgrn_lower_bounds(logits):
    p = jax.nn.softmax(logits.astype(F32), axis=1)
    cs = jnp.cumsum(p, axis=1)
    return cs - cs[:, :1]


def _s5_params(a_re, a_im, log_dt, b_re, b_im, c_re, c_im):
    dt = jnp.exp(log_dt)[..., None]
    mag = jnp.exp(a_re * dt)
    abar_re = mag * jnp.cos(a_im * dt)
    abar_im = mag * jnp.sin(a_im * dt)
    den = a_re * a_re + a_im * a_im
    num_re = abar_re - 1.0
    coef_re = ((num_re * a_re + abar_im * a_im) / den)[..., None]
    coef_im = ((abar_im * a_re - num_re * a_im) / den)[..., None]
    bbar_re = coef_re * b_re - coef_im * b_im
    bbar_im = coef_re * b_im + coef_im * b_re
    half = S5_GB * S5_STATE
    eye = jnp.eye(S5_GB, dtype=F32)
    a_bar = jnp.concatenate([abar_re.reshape(2, S5_NBLK, 1, half),
                             abar_im.reshape(2, S5_NBLK, 1, half)], axis=-1)

    def pack_b(bb):
        bb = bb.reshape(2, S5_NBLK, S5_GB, S5_STATE, S5_GROUP_CH).transpose(0, 1, 2, 4, 3)
        return jnp.einsum('dbgcp,gh->dbgchp', bb, eye).reshape(
            2, S5_NBLK, S5_GB * S5_GROUP_CH, half)

    def pack_c(cc):
        cc = cc.reshape(S5_NBLK, S5_GB, S5_GROUP_CH, S5_STATE).transpose(0, 1, 3, 2)
        return jnp.einsum('bgpc,gh->bgphc', cc, eye).reshape(S5_NBLK, half, S5_GB * S5_GROUP_CH)

    wb = jnp.concatenate([pack_b(bbar_re), pack_b(bbar_im)], axis=-1).astype(BF16)
    wc = jnp.concatenate([pack_c(c_re), -pack_c(c_im)], axis=1).astype(BF16)
    return a_bar, wb, wc


def _rope_tables(seq):
    rows = seq // GRID_W
    row = jnp.repeat(jnp.arange(rows, dtype=F32), GRID_W)
    col = jnp.tile(jnp.arange(GRID_W, dtype=F32), rows)
    quarter = HEAD // 4
    inv = ROPE_THETA ** (-jnp.arange(quarter, dtype=F32) / quarter)
    ang_r = row[:, None] * inv
    ang_c = col[:, None] * inv
    cos = jnp.concatenate([jnp.cos(ang_r)] * 2 + [jnp.cos(ang_c)] * 2, axis=-1)
    sin = jnp.concatenate([-jnp.sin(ang_r), jnp.sin(ang_r), -jnp.sin(ang_c), jnp.sin(ang_c)], axis=-1)
    return cos, sin


def _s5_state_in(s_re, s_im):
    def pack(s):
        b = s.shape[0]
        return s.reshape(b, 2, S5_NBLK, S5_GB * S5_STATE).transpose(1, 2, 0, 3)
    return jnp.concatenate([pack(s_re), pack(s_im)], axis=-1)


def _s5_state_out(fin):
    half = S5_GB * S5_STATE
    b = fin.shape[2]

    def unpack(s):
        return s.transpose(2, 0, 1, 3).reshape(b, 2, S5_GROUPS, S5_STATE)
    return unpack(fin[..., :half]), unpack(fin[..., half:])


def _trunk(x2, mod, mod_row, p, *, seq, depth, hg_s0, s5_s0, ctx_k, ctx_v, rope, want_state):
    hn = _first_norm(x2, mod[0], p['norm_w'][0], mod_row=mod_row)
    cos, sin = rope if rope is not None else (None, None)
    caches = hg_states = None
    if want_state:
        nseq = x2.shape[0] // seq
        hg_states = jnp.zeros((nseq, depth, 2, HG_HEADS, HEAD, HEAD), F32)
        caches = tuple(jnp.zeros((nseq, depth, seq, AT_KV_HEADS * HEAD), F32) for _ in range(2))
    s5_fins = []
    for l in range(depth):
        z = _in_proj(hn, p['w_in'], p['lower'], l)
        o_hg, hg_states = _hgrn(z, p['hg_onorm'][l], hg_s0, hg_states, l, depth,
                                seq=seq, write_state=want_state)
        o_at, caches = _attention(z, p['at_q_norm'][l], p['at_k_norm'][l], ctx_k, ctx_v, cos, sin, caches,
                                  l, depth, seq=seq)
        y, s5_fin = _s5_scan(z, p['s5'][l][1], p['s5'][l][0], p['s5'][l][2],
                             None if s5_s0 is None else s5_s0[l], seq=seq, write_state=want_state)
        s5_fins.append(s5_fin)
        o_s5 = _s5_post(y, z, p['s5_d'][l], p['s5_w_glu'], l)
        last = l == depth - 1
        res = _merge_out(o_hg, o_at, o_s5, z, x2, mod[l], p['w_br_hg'], p['w_br_at'], p['w_br_s5'],
                         p['w_out'], p['final_norm'] if last else p['norm_w'][l + 1],
                         None if last else mod[l + 1], l, mod_row=mod_row)
        if last:
            return res, caches, hg_states, s5_fins
        x2, hn = res


def kernel(x_prompt, x_sample, cache_k, cache_v, state_hgrn, state_s5_re, state_s5_im, c, c_ctx, norm_w, w_mod, b_mod, w_in, hg_lb_logits, hg_onorm, at_q_norm, at_k_norm, s5_a_re, s5_a_im, s5_log_dt, s5_b_re, s5_b_im, s5_c_re, s5_c_im, s5_d, s5_w_glu, w_br_hg, w_br_at, w_br_s5, w_out, final_norm):
    bsz_p, seq_p, d = x_prompt.shape
    bsz_s, seq_s, _ = x_sample.shape
    depth = w_in.shape[0]
    past = cache_k.shape[2]
    steps = S5_ROWS // S5_NB
    assert bsz_s == S5_NB and bsz_p % S5_NB == 0 and seq_p % steps == 0 and seq_s % steps == 0
    assert seq_p % HG_CHUNK == 0 and seq_s % HG_CHUNK == 0 and seq_p % TM_OUT == 0 and seq_s % TM_OUT == 0
    assert (bsz_p * seq_p) % TM_IN == 0 and (bsz_s * seq_s) % TM_IN == 0

    cvec = jnp.zeros((16, d), F32).at[:bsz_s].set(c).at[bsz_s].set(c_ctx)
    mod = _modulation(cvec, w_mod, b_mod).reshape(depth, 16, 3, d)

    lower = _hgrn_lower_bounds(hg_lb_logits)
    p = dict(
        norm_w=norm_w[:, None, :], final_norm=final_norm[None, :], w_in=w_in,
        lower=jnp.concatenate([lower[0], lower[1]], axis=-1)[:, None, :], hg_onorm=hg_onorm[:, None, :],
        at_q_norm=at_q_norm[:, None, :], at_k_norm=at_k_norm[:, None, :],
        s5=[_s5_params(s5_a_re[l], s5_a_im[l], s5_log_dt[l], s5_b_re[l], s5_b_im[l], s5_c_re[l], s5_c_im[l])
            for l in range(depth)],
        s5_d=s5_d[:, None, :], s5_w_glu=s5_w_glu.astype(BF16),
        w_br_hg=w_br_hg.astype(BF16), w_br_at=w_br_at.astype(BF16), w_br_s5=w_br_s5.astype(BF16),
        w_out=w_out.astype(BF16))

    yp, caches, hg_states, s5_fins = _trunk(
        x_prompt.reshape(bsz_p * seq_p, d), mod, lambda i: bsz_s, p, seq=seq_p, depth=depth,
        hg_s0=None, s5_s0=None, ctx_k=None, ctx_v=None, rope=None, want_state=True)
    y_prompt = yp.reshape(bsz_p, seq_p, d)
    new_cache_k = caches[0].reshape(bsz_p, depth, seq_p, AT_KV_HEADS, HEAD)
    new_cache_v = caches[1].reshape(bsz_p, depth, seq_p, AT_KV_HEADS, HEAD)
    s5s = [_s5_state_out(f) for f in s5_fins]
    new_state_s5_re = jnp.stack([s[0] for s in s5s], axis=1)
    new_state_s5_im = jnp.stack([s[1] for s in s5s], axis=1)

    tiles_per_seq = seq_s // TM_OUT
    ys, _, _, _ = _trunk(
        x_sample.reshape(bsz_s * seq_s, d), mod, lambda i: i // tiles_per_seq, p, seq=seq_s, depth=depth,
        hg_s0=state_hgrn,
        s5_s0=[_s5_state_in(state_s5_re[:, l], state_s5_im[:, l]) for l in range(depth)],
        ctx_k=cache_k.reshape(bsz_s, depth, past, AT_KV_HEADS * HEAD),
        ctx_v=cache_v.reshape(bsz_s, depth, past, AT_KV_HEADS * HEAD),
        rope=_rope_tables(seq_s), want_state=False)
    y_sample = ys.reshape(bsz_s, seq_s, d)

    return (y_prompt, y_sample, new_cache_k, new_cache_v, hg_states, new_state_s5_re, new_state_s5_im)
```

```python
import functools
import math

import jax
import jax.numpy as jnp
from jax import lax
from jax.experimental import pallas as pl
from jax.experimental.pallas import tpu as pltpu

F32 = jnp.float32
BF16 = jnp.bfloat16
EPS = 1e-6

D_MODEL = 2048
HG_WIDTH = 1024
HEAD = 128
HG_HEADS = 8
AT_KV_HEADS = 2
AT_GROUP = 4
S5_GROUPS = 64
S5_STATE = 64
S5_GROUP_CH = 16
GRID_W = 64
ROPE_THETA = 10000.0

COL_MERGE = 0
COL_HG = 6144
COL_S5U = 11264
COL_S5G = 12288
COL_ATQ = 13312
COL_ATG = 14336
COL_ATK = 15360
COL_ATV = 15616
IN_COLS = 15872

TM_IN = 2048
TN_IN = 512
TM_OUT = 256
HG_CHUNK = 64
HG_HB = 4
HG_FAST_MIN_LOG_DECAY = -60.0
S5_GB = 16
S5_NBLK = S5_GROUPS // S5_GB
S5_NB = 8
S5_ROWS = 2048
S5_PIECES = 8
VMEM_LIMIT = 56 * 1024 * 1024


def _sigmoid(x):
    return 1.0 / (1.0 + jnp.exp(-x))


def _dot(a, b):
    return jnp.dot(a, b, preferred_element_type=F32)


def _dot_nt(a, b):
    return lax.dot_general(a, b, (((1,), (1,)), ((), ())), preferred_element_type=F32)


def _dot_tn(a, b):
    return lax.dot_general(a, b, (((0,), (0,)), ((), ())), preferred_element_type=F32)


def _rms(x, w):
    ms = jnp.mean(x * x, axis=-1, keepdims=True)
    return x * lax.rsqrt(ms + EPS) * w


def _ada_norm(x, nw, mod_ref):
    return _rms(x, nw) * (1.0 + mod_ref[1:2, :]) + mod_ref[0:1, :]


def _mod_kernel(c_ref, w_ref, b_ref, o_ref):
    c = c_ref[...]
    s = (c * _sigmoid(c)).astype(BF16)
    o_ref[...] = _dot(s, w_ref[...].astype(BF16)) + b_ref[...]


def _modulation(cvec, w_mod, b_mod):
    depth, d, n = w_mod.shape
    tn = 1024
    return pl.pallas_call(
        _mod_kernel,
        grid=(depth, n // tn),
        in_specs=[pl.BlockSpec((16, d), lambda l, j: (0, 0)),
                  pl.BlockSpec((None, d, tn), lambda l, j: (l, 0, j)),
                  pl.BlockSpec((None, 1, tn), lambda l, j: (l, 0, j))],
        out_specs=pl.BlockSpec((None, 16, tn), lambda l, j: (l, 0, j)),
        out_shape=jax.ShapeDtypeStruct((depth, 16, n), F32),
        compiler_params=pltpu.CompilerParams(
            dimension_semantics=("arbitrary", "arbitrary"), vmem_limit_bytes=VMEM_LIMIT),
        name="modulation",
    )(cvec, w_mod, b_mod.reshape(depth, 1, n))


def _norm_kernel(x_ref, mod_ref, nw_ref, h_ref):
    h_ref[...] = _ada_norm(x_ref[...], nw_ref[...], mod_ref).astype(BF16)


def _first_norm(x2, mod3, norm_w, *, mod_row):
    n_tok, d = x2.shape
    tm = TM_OUT
    return pl.pallas_call(
        _norm_kernel,
        grid=(n_tok // tm,),
        in_specs=[pl.BlockSpec((tm, d), lambda i: (i, 0)),
                  pl.BlockSpec((None, 3, d), lambda i: (mod_row(i), 0, 0)),
                  pl.BlockSpec((1, d), lambda i: (0, 0))],
        out_specs=pl.BlockSpec((tm, d), lambda i: (i, 0)),
        out_shape=jax.ShapeDtypeStruct((n_tok, d), BF16),
        compiler_params=pltpu.CompilerParams(dimension_semantics=("arbitrary",)),
        name="first_norm",
    )(x2, mod3, norm_w)


def _w_in_tile(j):
    return jnp.where(j < 12, j + 19,
           jnp.where(j < 22, j - 12,
           jnp.where(j < 26, j - 7,
           jnp.where(j < 28, j - 16,
           jnp.where(j < 30, j - 15, 12)))))


TILE_FORGET0 = (COL_HG + 2 * HG_WIDTH) // TN_IN


def _in_tile_kind(j):
    def tiles(col):
        return col // TN_IN, (col + HG_WIDTH) // TN_IN

    def within(lo_hi):
        return (j >= lo_hi[0]) & (j < lo_hi[1])

    silu = within(tiles(COL_HG + 4 * HG_WIDTH)) | within(tiles(COL_S5G))
    forget = (j >= TILE_FORGET0) & (j < TILE_FORGET0 + 2 * HG_WIDTH // TN_IN)
    return jnp.where(forget, 1, jnp.where(silu, 2, 0))


def _in_proj_kernel(h_ref, w_ref, lb_ref, z_ref):
    def branch(kind):
        def fn():
            z = _dot(h_ref[...], w_ref[...].astype(BF16))
            if kind == 1:
                lb = lb_ref[...]
                z = lb + (1.0 - lb) * _sigmoid(z)
            elif kind == 2:
                z = z * _sigmoid(z)
            z_ref[...] = z
        return fn

    lax.switch(_in_tile_kind(pl.program_id(1)), [branch(kind) for kind in range(3)])


def _in_proj(hn, w_in, lower, layer):
    n_tok, d = hn.shape
    n_forget = 2 * HG_WIDTH // TN_IN
    return pl.pallas_call(
        _in_proj_kernel,
        grid=(n_tok // TM_IN, IN_COLS // TN_IN),
        in_specs=[pl.BlockSpec((TM_IN, d), lambda i, j: (i, 0)),
                  pl.BlockSpec((None, d, TN_IN), lambda i, j: (layer, 0, _w_in_tile(j))),
                  pl.BlockSpec((None, 1, TN_IN),
                               lambda i, j: (layer, 0, jnp.clip(j - TILE_FORGET0, 0, n_forget - 1)))],
        out_specs=pl.BlockSpec((TM_IN, TN_IN), lambda i, j: (i, j)),
        out_shape=jax.ShapeDtypeStruct((n_tok, IN_COLS), F32),
        compiler_params=pltpu.CompilerParams(
            dimension_semantics=("arbitrary", "arbitrary"), vmem_limit_bytes=VMEM_LIMIT),
        name="in_proj",
    )(hn, w_in, lower)


def _ref_rows(b, bs, ridx, sub8):
    c = b.shape[0]
    if bs >= 8:
        b3 = b.reshape(c // bs, bs, HEAD)
        return jnp.broadcast_to(b3[:, ridx:ridx + 1, :], (c // bs, bs, HEAD)).reshape(c, HEAD)
    b3 = b.reshape(c // 8, 8, HEAD)
    r = None
    for blk in range(8 // bs - 1, -1, -1):
        row = jnp.broadcast_to(b3[:, blk * bs + ridx:blk * bs + ridx + 1, :], (c // 8, 8, HEAD))
        r = row if r is None else jnp.where(sub8 < (blk + 1) * bs, row, r)
    return r.reshape(c, HEAD)


def _hgrn_scores_robust(q, k, f, b, rev, keep, xor, sub8):
    c = b.shape[0]
    scores = None
    for lvl in range(int(math.log2(c)), 1, -1):
        bs = 1 << lvl
        r = _ref_rows(b, bs, bs // 2 if rev else bs // 2 - 1, sub8)
        x = b - r
        e = jnp.exp(-jnp.abs(x))
        qe = jnp.where(x <= 0.0, e, 1.0)
        ke = jnp.where(x >= 0.0, e, 1.0)
        s_l = _dot_nt((q * qe).astype(BF16), (k * ke).astype(BF16))
        scores = s_l if scores is None else jnp.where(xor < bs, s_l, scores)
    s_l = _dot_nt((q * f).astype(BF16), k.astype(BF16))
    scores = jnp.where(xor < 2, s_l, scores)
    diag = jnp.sum(q * k, axis=1, keepdims=True)
    scores = jnp.where(xor == 0, diag, scores)
    return jnp.where(keep, scores, 0.0)


def _cumsum3(tri, g):
    w = g.shape[1]
    hi = g.astype(BF16)
    r1 = g - hi.astype(F32)
    mid = r1.astype(BF16)
    lo = (r1 - mid.astype(F32)).astype(BF16)
    r = _dot(tri, jnp.concatenate([hi, mid, lo], axis=1))
    return r[:, 0:w] + r[:, w:2 * w] + r[:, 2 * w:3 * w]


def _cumprod_rows(f, rev, row, seg):
    c = f.shape[0]
    pos = row % seg
    p = f
    s = 1
    while s < seg:
        shifted = pltpu.roll(p, (c - s) if rev else s, 0)
        valid = (pos < seg - s) if rev else (pos >= s)
        p = p * jnp.where(valid, shifted, 1.0)
        s *= 2
    return p


def _hgrn_kernel(*refs, seq, has_s0, write_state):
    q_ref, v_ref, ff_ref, fb_ref, g_ref, onw_ref = refs[:6]
    pos = 6
    s0_ref = st_ref = None
    if has_s0:
        s0_ref = refs[pos]
        pos += 1
    acc_ref, st_scr, low_ref = refs[-3:]
    if write_state:
        o_ref, st_ref = refs[-5:-3]
    else:
        o_ref = refs[-4]
    f_refs = (ff_ref, fb_ref)

    c = HG_CHUNK
    n = seq // c
    bw = HG_HB * HEAD
    ti = lax.broadcasted_iota(jnp.int32, (c, c), 0)
    si = lax.broadcasted_iota(jnp.int32, (c, c), 1)
    xor = jnp.bitwise_xor(ti, si)
    sub8 = lax.broadcasted_iota(jnp.int32, (c // 8, 8, HEAD), 1)
    row = lax.broadcasted_iota(jnp.int32, (c, bw), 0)
    keeps = (si <= ti, si >= ti)
    tris = tuple(jnp.where(kp, 1.0, 0.0).astype(BF16) for kp in keeps)
    heads = [slice(h * HEAD, (h + 1) * HEAD) for h in range(HG_HB)]

    for d in range(2):
        for i in range(n):
            f3 = f_refs[d][i * c:(i + 1) * c, :].reshape(c // 8, 8, bw)
            halves = []
            for j0 in (0, c // 16):
                prod = f3[j0]
                for j in range(j0 + 1, j0 + c // 16):
                    prod = prod * f3[j]
                for s in (4, 2, 1):
                    prod = prod * pltpu.roll(prod, s, 0)
                halves.append(prod)
            low_ref[d, i] = jnp.min(jnp.minimum(halves[0], halves[1]))
        for h in range(HG_HB):
            if has_s0:
                st_scr[d, h] = s0_ref[d, h].T
            else:
                st_scr[d, h] = jnp.zeros((HEAD, HEAD), F32)

    def body(ci, carry):
        rows = (pl.ds(pl.multiple_of(ci * c, c), c), pl.ds(pl.multiple_of((n - 1 - ci) * c, c), c))

        def run(fast):
            def fn():
                q = [q_ref[rows[d], :] for d in range(2)]
                f = [f_refs[d][rows[d], :] for d in range(2)]
                k = [1.0 - f[d] for d in range(2)]
                vb = [v_ref[rows[d], :].astype(BF16) for d in range(2)]
                scores = [[None] * HG_HB for _ in range(2)]
                if fast:
                    hc = c // 2
                    qd, kd, dec = [], [], []
                    for d in range(2):
                        p = _cumprod_rows(f[d], bool(d), row, hc)
                        near, far = (p[hc - 1:hc, :], p[c - 1:c, :]) if d == 0 else (p[hc:hc + 1, :], p[0:1, :])
                        in_far = (row >= hc) if d == 0 else (row < hc)
                        cum = p * jnp.where(in_far, near, 1.0)
                        kinv = k[d] / p
                        qh = (q[d] * p).astype(BF16)
                        qf = (q[d] * cum).astype(BF16)
                        kb = kinv.astype(BF16)
                        dec.append(near * far)
                        qd.append(qf)
                        kd.append((kinv * jnp.where(in_far, far, near * far)).astype(BF16))
                        same_half = (ti >= hc) == (si >= hc)
                        for h, hs in enumerate(heads):
                            r = _dot_nt(jnp.concatenate([qh[:, hs], qf[:, hs]], axis=0), kb[:, hs])
                            scores[d][h] = jnp.where(keeps[d], jnp.where(same_half, r[0:c], r[c:2 * c]), 0.0)
                else:
                    b = [_cumsum3(tris[d], jnp.log(f[d])) for d in range(2)]
                    edge = [b[0][c - 1:c, :], b[1][0:1, :]]
                    dec = [jnp.exp(edge[d]) for d in range(2)]
                    qd = [(q[d] * jnp.exp(b[d])).astype(BF16) for d in range(2)]
                    kd = [(k[d] * jnp.exp(edge[d] - b[d])).astype(BF16) for d in range(2)]
                    for d in range(2):
                        for h, hs in enumerate(heads):
                            scores[d][h] = _hgrn_scores_robust(
                                q[d][:, hs], k[d][:, hs], f[d][:, hs], b[d][:, hs], bool(d),
                                keeps[d], xor, sub8)
                for d in range(2):
                    for h, hs in enumerate(heads):
                        st = st_scr[d, h]
                        o = _dot(scores[d][h].astype(BF16), vb[d][:, hs]) + _dot_nt(
                            qd[d][:, hs], st.astype(BF16))
                        st_new = st * dec[d][:, hs] + _dot_tn(vb[d][:, hs], kd[d][:, hs])
                        st_scr[d, h] = st_new
                        acc_ref[d, rows[d], hs] = o
            return fn

        low = jnp.minimum(low_ref[0, ci], low_ref[1, n - 1 - ci])
        lax.cond(low >= math.exp(HG_FAST_MIN_LOG_DECAY), run(True), run(False))
        return carry

    lax.fori_loop(0, n, body, 0)
    if write_state:
        for d in range(2):
            for h in range(HG_HB):
                st_ref[d, h] = st_scr[d, h].T
    onw = onw_ref[...]
    for hs in heads:
        y = _rms(acc_ref[0, :, hs] + acc_ref[1, :, hs], onw)
        o_ref[:, hs] = (y * g_ref[:, hs]).astype(BF16)


def _hgrn(z, onorm_w, s0, states, layer, depth, *, seq, write_state):
    n_tok = z.shape[0]
    nseq = n_tok // seq
    bw = HG_HB * HEAD
    cb = COL_HG // bw
    hw = HG_WIDTH // bw

    def zspec(piece):
        return pl.BlockSpec((seq, bw), lambda b, h, p=piece: (b, cb + p * hw + h))

    in_specs = [zspec(0), zspec(1), zspec(2), zspec(3), zspec(4),
                pl.BlockSpec((1, HEAD), lambda b, h: (0, 0))]
    args = [z, z, z, z, z, onorm_w]
    if s0 is not None:
        in_specs.append(pl.BlockSpec((None, None, 2, HG_HB, HEAD, HEAD), lambda b, h: (b, layer, 0, h, 0, 0)))
        args.append(s0)
    out_specs = [pl.BlockSpec((seq, bw), lambda b, h: (b, h))]
    out_shape = [jax.ShapeDtypeStruct((n_tok, HG_WIDTH), BF16)]
    aliases = {}
    if write_state:
        out_specs.append(pl.BlockSpec((None, None, 2, HG_HB, HEAD, HEAD), lambda b, h: (b, layer, 0, h, 0, 0)))
        out_shape.append(jax.ShapeDtypeStruct((nseq, depth, 2, HG_HEADS, HEAD, HEAD), F32))
        if states is not None:
            aliases = {len(args): 1}
            in_specs.append(pl.BlockSpec(memory_space=pl.ANY))
            args.append(states)
    res = pl.pallas_call(
        functools.partial(_hgrn_kernel, seq=seq, has_s0=s0 is not None, write_state=write_state),
        grid=(nseq, HG_HEADS // HG_HB),
        in_specs=in_specs, out_specs=out_specs, out_shape=out_shape,
        input_output_aliases=aliases,
        scratch_shapes=[pltpu.VMEM((2, seq, bw), F32), pltpu.VMEM((2, HG_HB, HEAD, HEAD), F32),
                        pltpu.SMEM((2, seq // HG_CHUNK), F32)],
        compiler_params=pltpu.CompilerParams(
            dimension_semantics=("arbitrary", "arbitrary"), vmem_limit_bytes=VMEM_LIMIT),
        name="hgrn",
    )(*args)
    return (res[0], res[1]) if write_state else (res[0], None)


def _rope(x, cos, sin_signed, lane):
    swapped = jnp.where(lane % 64 < 32, pltpu.roll(x, 96, 1), pltpu.roll(x, 32, 1))
    return x * cos + swapped * sin_signed


def _attn_kernel(*refs, seq, past, ctx, qblk):
    q_ref, k_ref, v_ref, g_ref, qw_ref, kw_ref = refs[:6]
    pos = 6
    if ctx:
        ck_ref, cv_ref, cos_ref, sin_ref = refs[pos:pos + 4]
        o_ref = refs[-3]
    else:
        o_ref, nk_ref, nv_ref = refs[-5:-2]
    ks_ref, vs_ref = refs[-2:]

    k = _rms(k_ref[...], kw_ref[...])
    v = v_ref[...]
    vs_ref[:, HEAD:2 * HEAD] = jnp.ones((past + seq, HEAD), BF16)
    if ctx:
        lane = lax.broadcasted_iota(jnp.int32, (seq, HEAD), 1)
        ks_ref[0:past, :] = ck_ref[...].astype(BF16)
        vs_ref[0:past, 0:HEAD] = cv_ref[...].astype(BF16)
        k = _rope(k, cos_ref[...], sin_ref[...], lane)
    else:
        nk_ref[...] = k
        nv_ref[...] = v
    ks_ref[past:past + seq, :] = k.astype(BF16)
    vs_ref[past:past + seq, 0:HEAD] = v.astype(BF16)

    qscale = HEAD ** -0.5 * math.log2(math.e)
    qw = qw_ref[...]
    for hq in range(AT_GROUP):
        cols = slice(hq * HEAD, (hq + 1) * HEAD)
        for qb in range(seq // qblk):
            rows = slice(qb * qblk, (qb + 1) * qblk)
            q = _rms(q_ref[rows, cols], qw)
            if ctx:
                lane_q = lax.broadcasted_iota(jnp.int32, (qblk, HEAD), 1)
                q = _rope(q, cos_ref[rows, :], sin_ref[rows, :], lane_q)
            s = _dot_nt((q * qscale).astype(BF16), ks_ref[...])
            p = jnp.exp2(s - jnp.max(s, axis=-1, keepdims=True))
            res = _dot(p.astype(BF16), vs_ref[...])
            o = res[:, 0:HEAD] / res[:, HEAD:2 * HEAD]
            gt = g_ref[rows, cols]
            o_ref[rows, cols] = (o * (gt * _sigmoid(gt))).astype(BF16)


def _attention(z, q_norm, k_norm, ctx_k, ctx_v, cos, sin_signed, caches, layer, depth, *, seq):
    n_tok = z.shape[0]
    nseq = n_tok // seq
    ctx = ctx_k is not None
    past = ctx_k.shape[2] if ctx else 0
    gw = AT_GROUP * HEAD
    in_specs = [pl.BlockSpec((seq, gw), lambda b, h: (b, COL_ATQ // gw + h)),
                pl.BlockSpec((seq, HEAD), lambda b, h: (b, COL_ATK // HEAD + h)),
                pl.BlockSpec((seq, HEAD), lambda b, h: (b, COL_ATV // HEAD + h)),
                pl.BlockSpec((seq, gw), lambda b, h: (b, COL_ATG // gw + h)),
                pl.BlockSpec((1, HEAD), lambda b, h: (0, 0)),
                pl.BlockSpec((1, HEAD), lambda b, h: (0, 0))]
    args = [z, z, z, z, q_norm, k_norm]
    aliases = {}
    if ctx:
        kv_in = pl.BlockSpec((None, None, past, HEAD), lambda b, h: (b, layer, 0, h))
        in_specs += [kv_in, kv_in,
                     pl.BlockSpec((seq, HEAD), lambda b, h: (0, 0)),
                     pl.BlockSpec((seq, HEAD), lambda b, h: (0, 0))]
        args += [ctx_k, ctx_v, cos, sin_signed]
    out_specs = [pl.BlockSpec((seq, gw), lambda b, h: (b, h))]
    out_shape = [jax.ShapeDtypeStruct((n_tok, AT_KV_HEADS * gw), BF16)]
    if not ctx:
        kv_spec = pl.BlockSpec((None, None, seq, HEAD), lambda b, h: (b, layer, 0, h))
        out_specs += [kv_spec, kv_spec]
        out_shape += [jax.ShapeDtypeStruct((nseq, depth, seq, AT_KV_HEADS * HEAD), F32)] * 2
        if caches is not None:
            aliases = {len(args): 1, len(args) + 1: 2}
            in_specs += [pl.BlockSpec(memory_space=pl.ANY)] * 2
            args += list(caches)
    res = pl.pallas_call(
        functools.partial(_attn_kernel, seq=seq, past=past, ctx=ctx, qblk=min(seq, 256)),
        grid=(nseq, AT_KV_HEADS),
        in_specs=in_specs, out_specs=out_specs, out_shape=out_shape,
        input_output_aliases=aliases,
        scratch_shapes=[pltpu.VMEM((past + seq, HEAD), BF16), pltpu.VMEM((past + seq, 2 * HEAD), BF16)],
        compiler_params=pltpu.CompilerParams(
            dimension_semantics=("arbitrary", "arbitrary"), vmem_limit_bytes=VMEM_LIMIT),
        name="attention",
    )(*args)
    return (res[0], (res[1], res[2])) if not ctx else (res[0], None)


def _s5_scan_kernel(*refs, has_s0, write_state):
    u_ref, wb_ref, a_ref, wc_ref = refs[:4]
    pos = 4
    s0_ref = fin_ref = None
    if has_s0:
        s0_ref = refs[pos]
        pos += 1
    y_ref = refs[pos]
    pos += 1
    if write_state:
        fin_ref = refs[pos]
    carry_ref, s_ref = refs[-2:]

    d = pl.program_id(2)
    tc = pl.program_id(3)
    nt = pl.num_programs(3)
    half = S5_GB * S5_STATE
    uw = S5_GB * S5_GROUP_CH
    steps = S5_ROWS // S5_NB
    ps = steps // S5_PIECES
    pr = S5_ROWS // S5_PIECES

    @pl.when(tc == 0)
    def _():
        if has_s0:
            carry_ref[...] = s0_ref[...]
        else:
            carry_ref[...] = jnp.zeros_like(carry_ref)

    def run(rev):
        def fn():
            u = pltpu.einshape("btc->tbc", u_ref[...]).reshape(S5_ROWS, uw).astype(BF16)
            a_re = jnp.broadcast_to(a_ref[:, 0:half], (S5_NB, half))
            a_im = jnp.broadcast_to(a_ref[:, half:2 * half], (S5_NB, half))
            t0 = ((nt - 1 - tc) if rev else tc) * steps
            order = list(range(S5_PIECES))[::-1] if rev else list(range(S5_PIECES))

            def project_in(p):
                s_ref[p * pr:(p + 1) * pr, :] = _dot(u[p * pr:(p + 1) * pr, :], wb_ref[...])

            def recur(p, carry):
                sr, sm = carry
                ts = range(p * ps, (p + 1) * ps)
                for t in (reversed(ts) if rev else ts):
                    rows = slice(t * S5_NB, (t + 1) * S5_NB)
                    sr, sm = (a_re * sr - a_im * sm + s_ref[rows, 0:half],
                              a_re * sm + a_im * sr + s_ref[rows, half:2 * half])
                    s_ref[rows, 0:half] = sr
                    s_ref[rows, half:2 * half] = sm
                return sr, sm

            def project_out(p):
                y = _dot(s_ref[p * pr:(p + 1) * pr, :].astype(BF16), wc_ref[...])
                y = pltpu.einshape("tbc->btc", y.reshape(ps, S5_NB, uw))
                ts = pl.ds(pl.multiple_of(t0 + p * ps, ps), ps)
                if rev:
                    y_ref[:, ts, :] = y_ref[:, ts, :] + y
                else:
                    y_ref[:, ts, :] = y

            carry = (carry_ref[:, 0:half], carry_ref[:, half:2 * half])
            project_in(order[0])
            for i, p in enumerate(order):
                if i + 1 < S5_PIECES:
                    project_in(order[i + 1])
                carry = recur(p, carry)
                if i > 0:
                    project_out(order[i - 1])
            project_out(order[-1])
            carry_ref[:, 0:half] = carry[0]
            carry_ref[:, half:2 * half] = carry[1]
        return fn

    lax.cond(d == 0, run(False), run(True))

    if write_state:
        @pl.when(tc == nt - 1)
        def _():
            fin_ref[...] = carry_ref[...]


def _s5_scan(z, wb, a_bar, wc, s0, *, seq, write_state):
    n_tok = z.shape[0]
    nseq = n_tok // seq
    steps = S5_ROWS // S5_NB
    nt = seq // steps
    uw = S5_GB * S5_GROUP_CH
    sw = 2 * S5_GB * S5_STATE

    def tblk(d, t):
        return jnp.where(d == 0, t, nt - 1 - t)

    st_spec = pl.BlockSpec((None, None, S5_NB, sw), lambda g, b, d, t: (d, g, b, 0))
    in_specs = [pl.BlockSpec((S5_NB, steps, uw), lambda g, b, d, t: (b, tblk(d, t), COL_S5U // uw + g)),
                pl.BlockSpec((None, None, uw, sw), lambda g, b, d, t: (d, g, 0, 0)),
                pl.BlockSpec((None, None, 1, sw), lambda g, b, d, t: (d, g, 0, 0)),
                pl.BlockSpec((None, sw, uw), lambda g, b, d, t: (g, 0, 0))]
    args = [z.reshape(nseq, seq, IN_COLS), wb, a_bar, wc]
    if s0 is not None:
        in_specs.append(st_spec)
        args.append(s0)
    out_specs = [pl.BlockSpec((S5_NB, seq, uw), lambda g, b, d, t: (b, 0, g))]
    out_shape = [jax.ShapeDtypeStruct((nseq, seq, HG_WIDTH), F32)]
    if write_state:
        out_specs.append(st_spec)
        out_shape.append(jax.ShapeDtypeStruct((2, S5_NBLK, nseq, sw), F32))
    res = pl.pallas_call(
        functools.partial(_s5_scan_kernel, has_s0=s0 is not None, write_state=write_state),
        grid=(S5_NBLK, nseq // S5_NB, 2, nt),
        in_specs=in_specs, out_specs=out_specs, out_shape=out_shape,
        scratch_shapes=[pltpu.VMEM((S5_NB, sw), F32), pltpu.VMEM((S5_ROWS, sw), F32)],
        compiler_params=pltpu.CompilerParams(
            dimension_semantics=("arbitrary", "arbitrary", "arbitrary", "arbitrary"),
            vmem_limit_bytes=VMEM_LIMIT),
        name="s5_scan",
    )(*args)
    y = res[0].reshape(n_tok, HG_WIDTH)
    return (y, res[1]) if write_state else (y, None)


def _s5_post_kernel(y_ref, u_ref, g_ref, d_ref, w_ref, o_ref):
    y = y_ref[...] + d_ref[...] * u_ref[...]
    y = 0.5 * y * (1.0 + jnp.tanh(math.sqrt(2.0 / math.pi) * (y + 0.044715 * (y * y * y))))
    y = y * _sigmoid(_dot(y.astype(BF16), w_ref[...]))
    o_ref[...] = (y * g_ref[...]).astype(BF16)


def _s5_post(y, z, d_skip, w_glu, layer):
    n_tok = z.shape[0]
    w = HG_WIDTH
    tm = 1024
    return pl.pallas_call(
        _s5_post_kernel,
        grid=(n_tok // tm,),
        in_specs=[pl.BlockSpec((tm, w), lambda i: (i, 0)),
                  pl.BlockSpec((tm, w), lambda i: (i, COL_S5U // w)),
                  pl.BlockSpec((tm, w), lambda i: (i, COL_S5G // w)),
                  pl.BlockSpec((1, w), lambda i: (0, 0)),
                  pl.BlockSpec((None, w, w), lambda i: (layer, 0, 0))],
        out_specs=pl.BlockSpec((tm, w), lambda i: (i, 0)),
        out_shape=jax.ShapeDtypeStruct((n_tok, w), BF16),
        compiler_params=pltpu.CompilerParams(
            dimension_semantics=("arbitrary",), vmem_limit_bytes=VMEM_LIMIT),
        name="s5_post",
    )(y, z, z, d_skip, w_glu)


def _merge_kernel(ohg_ref, oat_ref, os5_ref, m0_ref, m1_ref, m2_ref, x_ref, mod_ref,
                  wh_ref, wa_ref, ws_ref, wo_ref, nw_ref, *rest, last):
    merged = _sigmoid(m0_ref[...]) * _dot(ohg_ref[...], wh_ref[...])
    merged = merged + _sigmoid(m1_ref[...]) * _dot(oat_ref[...], wa_ref[...])
    merged = merged + _sigmoid(m2_ref[...]) * _dot(os5_ref[...], ws_ref[...])
    out = _dot(merged.astype(BF16), wo_ref[...])
    x_new = x_ref[...] + mod_ref[2:3, :] * out
    if last:
        (y_ref,) = rest
        y_ref[...] = _rms(x_new, nw_ref[...])
    else:
        modn_ref, xo_ref, hn_ref = rest
        xo_ref[...] = x_new
        hn_ref[...] = _ada_norm(x_new, nw_ref[...], modn_ref).astype(BF16)


def _merge_out(o_hg, o_at, o_s5, z, x2, mod3, w_hg, w_at, w_s5, w_out, next_norm_w, mod3_next,
               layer, *, mod_row):
    n_tok, d = x2.shape
    w = HG_WIDTH
    tm = TM_OUT
    last = mod3_next is None
    act = pl.BlockSpec((tm, w), lambda i: (i, 0))
    row = pl.BlockSpec((tm, d), lambda i: (i, 0))
    mod_spec = pl.BlockSpec((None, 3, d), lambda i: (mod_row(i), 0, 0))

    def wspec(rows):
        return pl.BlockSpec((None, rows, d), lambda i: (layer, 0, 0), pipeline_mode=pl.Buffered(1))

    in_specs = [act, act, act,
                pl.BlockSpec((tm, d), lambda i: (i, 0)),
                pl.BlockSpec((tm, d), lambda i: (i, 1)),
                pl.BlockSpec((tm, d), lambda i: (i, 2)),
                row, mod_spec, wspec(w), wspec(w), wspec(w), wspec(d),
                pl.BlockSpec((1, d), lambda i: (0, 0))]
    args = [o_hg, o_at, o_s5, z, z, z, x2, mod3, w_hg, w_at, w_s5, w_out, next_norm_w]
    if last:
        out_specs = row
        out_shape = jax.ShapeDtypeStruct((n_tok, d), F32)
    else:
        in_specs.append(mod_spec)
        args.append(mod3_next)
        out_specs = [row, row]
        out_shape = [jax.ShapeDtypeStruct((n_tok, d), F32), jax.ShapeDtypeStruct((n_tok, d), BF16)]
    return pl.pallas_call(
        functools.partial(_merge_kernel, last=last),
        grid=(n_tok // tm,),
        in_specs=in_specs, out_specs=out_specs, out_shape=out_shape,
        compiler_params=pltpu.CompilerParams(
            dimension_semantics=("arbitrary",), vmem_limit_bytes=VMEM_LIMIT),
        name="merge_out",
    )(*args)


def _hgrn_lower_bounds(logits):
    p = jax.nn.softmax(logits.astype(F32), axis=1)
    cs = jnp.cumsum(p, axis=1)
    return cs - cs[:, :1]


def _s5_params(a_re, a_im, log_dt, b_re, b_im, c_re, c_im):
    dt = jnp.exp(log_dt)[..., None]
    mag = jnp.exp(a_re * dt)
    abar_re = mag * jnp.cos(a_im * dt)
    abar_im = mag * jnp.sin(a_im * dt)
    den = a_re * a_re + a_im * a_im
    num_re = abar_re - 1.0
    coef_re = ((num_re * a_re + abar_im * a_im) / den)[..., None]
    coef_im = ((abar_im * a_re - num_re * a_im) / den)[..., None]
    bbar_re = coef_re * b_re - coef_im * b_im
    bbar_im = coef_re * b_im + coef_im * b_re
    half = S5_GB * S5_STATE
    eye = jnp.eye(S5_GB, dtype=F32)
    a_bar = jnp.concatenate([abar_re.reshape(2, S5_NBLK, 1, half),
                             abar_im.reshape(2, S5_NBLK, 1, half)], axis=-1)

    def pack_b(bb):
        bb = bb.reshape(2, S5_NBLK, S5_GB, S5_STATE, S5_GROUP_CH).transpose(0, 1, 2, 4, 3)
        return jnp.einsum('dbgcp,gh->dbgchp', bb, eye).reshape(
            2, S5_NBLK, S5_GB * S5_GROUP_CH, half)

    def pack_c(cc):
        cc = cc.reshape(S5_NBLK, S5_GB, S5_GROUP_CH, S5_STATE).transpose(0, 1, 3, 2)
        return jnp.einsum('bgpc,gh->bgphc', cc, eye).reshape(S5_NBLK, half, S5_GB * S5_GROUP_CH)

    wb = jnp.concatenate([pack_b(bbar_re), pack_b(bbar_im)], axis=-1).astype(BF16)
    wc = jnp.concatenate([pack_c(c_re), -pack_c(c_im)], axis=1).astype(BF16)
    return a_bar, wb, wc


def _rope_tables(seq):
    rows = seq // GRID_W
    row = jnp.repeat(jnp.arange(rows, dtype=F32), GRID_W)
    col = jnp.tile(jnp.arange(GRID_W, dtype=F32), rows)
    quarter = HEAD // 4
    inv = ROPE_THETA ** (-jnp.arange(quarter, dtype=F32) / quarter)
    ang_r = row[:, None] * inv
    ang_c = col[:, None] * inv
    cos = jnp.concatenate([jnp.cos(ang_r)] * 2 + [jnp.cos(ang_c)] * 2, axis=-1)
    sin = jnp.concatenate([-jnp.sin(ang_r), jnp.sin(ang_r), -jnp.sin(ang_c), jnp.sin(ang_c)], axis=-1)
    return cos, sin


def _s5_state_in(s_re, s_im):
    def pack(s):
        b = s.shape[0]
        return s.reshape(b, 2, S5_NBLK, S5_GB * S5_STATE).transpose(1, 2, 0, 3)
    return jnp.concatenate([pack(s_re), pack(s_im)], axis=-1)


def _s5_state_out(fin):
    half = S5_GB * S5_STATE
    b = fin.shape[2]

    def unpack(s):
        return s.transpose(2, 0, 1, 3).reshape(b, 2, S5_GROUPS, S5_STATE)
    return unpack(fin[..., :half]), unpack(fin[..., half:])


def _trunk(x2, mod, mod_row, p, *, seq, depth, hg_s0, s5_s0, ctx_k, ctx_v, rope, want_state):
    hn = _first_norm(x2, mod[0], p['norm_w'][0], mod_row=mod_row)
    cos, sin = rope if rope is not None else (None, None)
    caches = hg_states = None
    if want_state:
        nseq = x2.shape[0] // seq
        hg_states = jnp.zeros((nseq, depth, 2, HG_HEADS, HEAD, HEAD), F32)
        caches = tuple(jnp.zeros((nseq, depth, seq, AT_KV_HEADS * HEAD), F32) for _ in range(2))
    s5_fins = []
    for l in range(depth):
        z = _in_proj(hn, p['w_in'], p['lower'], l)
        o_hg, hg_states = _hgrn(z, p['hg_onorm'][l], hg_s0, hg_states, l, depth,
                                seq=seq, write_state=want_state)
        o_at, caches = _attention(z, p['at_q_norm'][l], p['at_k_norm'][l], ctx_k, ctx_v, cos, sin, caches,
                                  l, depth, seq=seq)
        y, s5_fin = _s5_scan(z, p['s5'][l][1], p['s5'][l][0], p['s5'][l][2],
                             None if s5_s0 is None else s5_s0[l], seq=seq, write_state=want_state)
        s5_fins.append(s5_fin)
        o_s5 = _s5_post(y, z, p['s5_d'][l], p['s5_w_glu'], l)
        last = l == depth - 1
        res = _merge_out(o_hg, o_at, o_s5, z, x2, mod[l], p['w_br_hg'], p['w_br_at'], p['w_br_s5'],
                         p['w_out'], p['final_norm'] if last else p['norm_w'][l + 1],
                         None if last else mod[l + 1], l, mod_row=mod_row)
        if last:
            return res, caches, hg_states, s5_fins
        x2, hn = res


def kernel(x_prompt, x_sample, cache_k, cache_v, state_hgrn, state_s5_re, state_s5_im, c, c_ctx, norm_w, w_mod, b_mod, w_in, hg_lb_logits, hg_onorm, at_q_norm, at_k_norm, s5_a_re, s5_a_im, s5_log_dt, s5_b_re, s5_b_im, s5_c_re, s5_c_im, s5_d, s5_w_glu, w_br_hg, w_br_at, w_br_s5, w_out, final_norm):
    bsz_p, seq_p, d = x_prompt.shape
    bsz_s, seq_s, _ = x_sample.shape
    depth = w_in.shape[0]
    past = cache_k.shape[2]
    steps = S5_ROWS // S5_NB
    assert bsz_s == S5_NB and bsz_p % S5_NB == 0 and seq_p % steps == 0 and seq_s % steps == 0
    assert seq_p % HG_CHUNK == 0 and seq_s % HG_CHUNK == 0 and seq_p % TM_OUT == 0 and seq_s % TM_OUT == 0
    assert (bsz_p * seq_p) % TM_IN == 0 and (bsz_s * seq_s) % TM_IN == 0

    cvec = jnp.zeros((16, d), F32).at[:bsz_s].set(c).at[bsz_s].set(c_ctx)
    mod = _modulation(cvec, w_mod, b_mod).reshape(depth, 16, 3, d)

    lower = _hgrn_lower_bounds(hg_lb_logits)
    p = dict(
        norm_w=norm_w[:, None, :], final_norm=final_norm[None, :], w_in=w_in,
        lower=jnp.concatenate([lower[0], lower[1]], axis=-1)[:, None, :], hg_onorm=hg_onorm[:, None, :],
        at_q_norm=at_q_norm[:, None, :], at_k_norm=at_k_norm[:, None, :],
        s5=[_s5_params(s5_a_re[l], s5_a_im[l], s5_log_dt[l], s5_b_re[l], s5_b_im[l], s5_c_re[l], s5_c_im[l])
            for l in range(depth)],
        s5_d=s5_d[:, None, :], s5_w_glu=s5_w_glu.astype(BF16),
        w_br_hg=w_br_hg.astype(BF16), w_br_at=w_br_at.astype(BF16), w_br_s5=w_br_s5.astype(BF16),
        w_out=w_out.astype(BF16))

    yp, caches, hg_states, s5_fins = _trunk(
        x_prompt.reshape(bsz_p * seq_p, d), mod, lambda i: bsz_s, p, seq=seq_p, depth=depth,
        hg_s0=None, s5_s0=None, ctx_k=None, ctx_v=None, rope=None, want_state=True)
    y_prompt = yp.reshape(bsz_p, seq_p, d)
    new_cache_k = caches[0].reshape(bsz_p, depth, seq_p, AT_KV_HEADS, HEAD)
    new_cache_v = caches[1].reshape(bsz_p, depth, seq_p, AT_KV_HEADS, HEAD)
    s5s = [_s5_state_out(f) for f in s5_fins]
    new_state_s5_re = jnp.stack([s[0] for s in s5s], axis=1)
    new_state_s5_im = jnp.stack([s[1] for s in s5s], axis=1)

    tiles_per_seq = seq_s // TM_OUT
    ys, _, _, _ = _trunk(
        x_sample.reshape(bsz_s * seq_s, d), mod, lambda i: i // tiles_per_seq, p, seq=seq_s, depth=depth,
        hg_s0=state_hgrn,
        s5_s0=[_s5_state_in(state_s5_re[:, l], state_s5_im[:, l]) for l in range(depth)],
        ctx_k=cache_k.reshape(bsz_s, depth, past, AT_KV_HEADS * HEAD),
        ctx_v=cache_v.reshape(bsz_s, depth, past, AT_KV_HEADS * HEAD),
        rope=_rope_tables(seq_s), want_state=False)
    y_sample = ys.reshape(bsz_s, seq_s, d)

    return (y_prompt, y_sample, new_cache_k, new_cache_v, hg_states, new_state_s5_re, new_state_s5_im)
```
